```python
import jax, jax.numpy as jnp
from jax import lax
import numpy as np


D_MODEL = 1024
BATCH = 4
SEQ = 4096
DEPTH = 1

EPS = 1e-6
D_FF = 2816
FFN_RES_WEIGHT = 0.5
CONV_DIM = D_MODEL
CONV_K = 3
GLA_HEADS = 4
GLA_DK = D_MODEL // 2 // GLA_HEADS
GLA_DV = D_MODEL // GLA_HEADS
GLA_QK = GLA_HEADS * GLA_DK
GLA_V = GLA_HEADS * GLA_DV
GATE_RANK = 16
GATE_TAU = 16.0
CHUNK = 64
IN_SIZES = (CONV_DIM, CONV_DIM, CONV_DIM, GLA_QK, GLA_QK, GLA_V, GATE_RANK, GLA_V, D_MODEL, D_MODEL)
IN_COLS = sum(IN_SIZES)
SPLIT_POINTS = tuple(int(s) for s in np.cumsum(IN_SIZES)[:-1])

kernel_name = 'hybrid_conv_gla_macaron_block'


def rmsnorm(x, g):
    xf = x.astype(jnp.float32)
    y = xf * lax.rsqrt(jnp.mean(xf * xf, axis=-1, keepdims=True) + EPS)
    return (y * g.astype(jnp.float32)).astype(x.dtype)


def swiglu(x, w_gu, w_down):
    g, u = jnp.split(x @ w_gu, 2, axis=-1)
    return (jax.nn.silu(g) * u) @ w_down


def causal_short_conv(u, w):
    S = u.shape[1]
    up = jnp.pad(u, ((0, 0), (CONV_K - 1, 0), (0, 0)))
    out = up[:, 0:S] * w[0]
    for j in range(1, CONV_K):
        out = out + up[:, j:j + S] * w[j]
    return out


def gla_chunked(q, k, v, log_a):
    B, S = q.shape[0], q.shape[1]
    N = S // CHUNK
    f32 = jnp.float32

    def to_chunks(t):
        return t.reshape(B, N, CHUNK, GLA_HEADS, t.shape[-1]).transpose(0, 3, 1, 2, 4).astype(f32)

    qc = to_chunks(q) * (GLA_DK ** -0.5)
    kc, vc, gc = to_chunks(k), to_chunks(v), to_chunks(log_a)
    b = jnp.cumsum(gc, axis=3)
    b_last = b[:, :, :, -1:]
    b_ref = b[:, :, :, CHUNK // 2 - 1:CHUNK // 2]

    q_ref = qc * jnp.exp(b - b_ref)
    k_ref = kc * jnp.exp(b_ref - b)
    scores = jnp.einsum('bhnid,bhnjd->bhnij', q_ref, k_ref)
    causal = jnp.tril(jnp.ones((CHUNK, CHUNK), dtype=bool))
    scores = jnp.where(causal, scores, 0.0)
    o_intra = jnp.einsum('bhnij,bhnjv->bhniv', scores, vc)

    q_inter = q_ref * jnp.exp(b_ref)
    k_state = kc * jnp.exp(b_last - b)
    kv = jnp.einsum('bhncd,bhncv->nbhdv', k_state, vc)
    decay = jnp.exp(b_last[:, :, :, 0]).transpose(2, 0, 1, 3)

    def step(state, inp):
        dec, kv_n = inp
        return state * dec[..., None] + kv_n, state

    init = jnp.zeros((B, GLA_HEADS, GLA_DK, GLA_DV), f32)
    _, states = lax.scan(step, init, (decay, kv))
    o_inter = jnp.einsum('bhncd,nbhdv->bhncv', q_inter, states)
    o = o_intra + o_inter
    return o.transpose(0, 2, 3, 1, 4).reshape(B, S, GLA_HEADS, GLA_DV)


def hybrid_mixer(h, w_in, conv_w, gate_w_up, gate_b, gla_norm_g, w_out_conv, w_out_gla, w_o):
    B, S, _ = h.shape
    z = h @ w_in
    cb, cc, cu, q, k, v, g_lr, r, ga, gb = jnp.split(z, SPLIT_POINTS, axis=-1)

    y_conv = (cb * causal_short_conv(cc * cu, conv_w)) @ w_out_conv

    log_a = jax.nn.log_sigmoid((g_lr @ gate_w_up + gate_b).astype(jnp.float32)) / GATE_TAU
    o = gla_chunked(q.reshape(B, S, GLA_HEADS, GLA_DK),
                    k.reshape(B, S, GLA_HEADS, GLA_DK),
                    v.reshape(B, S, GLA_HEADS, GLA_DV),
                    log_a.reshape(B, S, GLA_HEADS, GLA_DK))
    o = o * lax.rsqrt(jnp.mean(o * o, axis=-1, keepdims=True) + EPS)
    o = o.reshape(B, S, GLA_V) * gla_norm_g.astype(jnp.float32)
    y_gla = (o.astype(h.dtype) * jax.nn.silu(r)) @ w_out_gla

    merged = jax.nn.sigmoid(ga) * y_conv + jax.nn.sigmoid(gb) * y_gla
    return merged @ w_o


def setup_inputs(seed: int = 0) -> dict:
    key = jax.random.key(seed)
    ks = jax.random.split(key, 20)
    f32 = jnp.float32

    def nrm(k, shape, fan_in):
        return jax.random.normal(k, shape, f32) * (fan_in ** -0.5)

    def gain(k, shape):
        return 1.0 + 0.02 * jax.random.normal(k, shape, f32)

    L = DEPTH
    return {
        'x': jax.random.normal(ks[0], (BATCH, SEQ, D_MODEL), f32),
        'norm_ffn1_g': gain(ks[1], (L, D_MODEL)),
        'ffn1_w_gu': nrm(ks[2], (L, D_MODEL, 2 * D_FF), D_MODEL),
        'ffn1_w_down': nrm(ks[3], (L, D_FF, D_MODEL), D_FF),
        'norm_mix_g': gain(ks[4], (L, D_MODEL)),
        'w_in': nrm(ks[5], (L, D_MODEL, IN_COLS), D_MODEL),
        'conv_w': nrm(ks[6], (L, CONV_K, CONV_DIM), CONV_K),
        'gate_w_up': nrm(ks[7], (L, GATE_RANK, GLA_QK), GATE_RANK),
        'gate_b': 0.1 * jax.random.normal(ks[8], (L, GLA_QK), f32),
        'gla_norm_g': gain(ks[9], (L, GLA_V)),
        'w_out_conv': nrm(ks[10], (L, CONV_DIM, D_MODEL), CONV_DIM),
        'w_out_gla': nrm(ks[11], (L, GLA_V, D_MODEL), GLA_V),
        'w_o': nrm(ks[12], (L, D_MODEL, D_MODEL), D_MODEL),
        'norm_ffn2_g': gain(ks[13], (L, D_MODEL)),
        'ffn2_w_gu': nrm(ks[14], (L, D_MODEL, 2 * D_FF), D_MODEL),
        'ffn2_w_down': nrm(ks[15], (L, D_FF, D_MODEL), D_FF),
        'final_norm_g': gain(ks[16], (D_MODEL,)),
    }


def reference(x, norm_ffn1_g, ffn1_w_gu, ffn1_w_down, norm_mix_g, w_in, conv_w, gate_w_up, gate_b,
              gla_norm_g, w_out_conv, w_out_gla, w_o, norm_ffn2_g, ffn2_w_gu, ffn2_w_down, final_norm_g):
    for l in range(DEPTH):
        x = x + FFN_RES_WEIGHT * swiglu(rmsnorm(x, norm_ffn1_g[l]), ffn1_w_gu[l], ffn1_w_down[l])
        x = x + hybrid_mixer(rmsnorm(x, norm_mix_g[l]), w_in[l], conv_w[l], gate_w_up[l], gate_b[l],
                             gla_norm_g[l], w_out_conv[l], w_out_gla[l], w_o[l])
        x = x + FFN_RES_WEIGHT * swiglu(rmsnorm(x, norm_ffn2_g[l]), ffn2_w_gu[l], ffn2_w_down[l])
    return rmsnorm(x, final_norm_g)
```

```python
import functools

import jax
import jax.numpy as jnp
from jax import lax
from jax.experimental import pallas as pl
from jax.experimental.pallas import tpu as pltpu

D_MODEL = 1024
SEQ = 4096
D_FF = 2816
EPS = 1e-6
FFN_RES_WEIGHT = 0.5
CONV_K = 3
GLA_HEADS = 4
GLA_DK = 128
GLA_DV = 256
GLA_QK = GLA_HEADS * GLA_DK
GLA_V = GLA_HEADS * GLA_DV
GATE_RANK = 16
GATE_TAU = 16.0
CHUNK = 64
PAIR = 2 * CHUNK
LANES = 128
CUM_BLOCK = 256
KG_ROWS = GLA_QK + LANES

_OFF_CONV = 0
_OFF_Q = 3 * D_MODEL
_OFF_K = _OFF_Q + GLA_QK
_OFF_V = _OFF_K + GLA_QK
_OFF_GLR = _OFF_V + GLA_V
_OFF_R = _OFF_GLR + GATE_RANK
_OFF_END = _OFF_R + 3 * D_MODEL

TM_FFN = 512
TM_MIX = 256
VMEM_LIMIT_BYTES = 56 * 1024 * 1024

F32 = jnp.float32
BF16 = jnp.bfloat16


def _rmsnorm(x, g):
    return x * lax.rsqrt(jnp.mean(x * x, axis=-1, keepdims=True) + EPS) * g


def _dot(a, b):
    return jnp.dot(a, b, preferred_element_type=F32)


def _dot_nt(a, b):
    return lax.dot_general(a, b, (((1,), (1,)), ((), ())), preferred_element_type=F32)


def _log_sigmoid(x):
    return jnp.minimum(x, 0.0) - jnp.log1p(jnp.exp(-jnp.abs(x)))


def _split_hi_lo(x):
    hi = x.astype(BF16)
    lo = (x - hi.astype(F32)).astype(BF16)
    return hi, lo


def _ffn_kernel(x_ref, g_ref, wgu_ref, wd_ref, fg_ref, o_ref, a_ref, *, final_norm):
    x = x_ref[...]
    h = _rmsnorm(x, g_ref[...]).astype(BF16)
    nb = 256
    for c in range(D_FF // nb):
        g = _dot(h, wgu_ref[:, c * nb:(c + 1) * nb])
        u = _dot(h, wgu_ref[:, D_FF + c * nb:D_FF + (c + 1) * nb])
        a_ref[:, c * nb:(c + 1) * nb] = (g * jax.nn.sigmoid(g) * u).astype(BF16)
    y = x + FFN_RES_WEIGHT * _dot(a_ref[...], wd_ref[...])
    if final_norm:
        y = _rmsnorm(y, fg_ref[...])
    o_ref[...] = y


def _resident(shape):
    return pl.BlockSpec(shape, lambda i: (0,) * len(shape), pipeline_mode=pl.Buffered(1))


def _ffn(x, norm_g, w_gu, w_down, final_g, *, final_norm, name):
    t = x.shape[0]
    tile = pl.BlockSpec((TM_FFN, D_MODEL), lambda i: (i, 0))
    return pl.pallas_call(
        functools.partial(_ffn_kernel, final_norm=final_norm),
        grid=(t // TM_FFN,),
        in_specs=[tile, _resident((1, D_MODEL)), _resident((D_MODEL, 2 * D_FF)),
                  _resident((D_FF, D_MODEL)), _resident((1, D_MODEL))],
        out_specs=tile,
        out_shape=jax.ShapeDtypeStruct((t, D_MODEL), F32),
        scratch_shapes=[pltpu.VMEM((TM_FFN, D_FF), BF16)],
        compiler_params=pltpu.CompilerParams(
            dimension_semantics=("arbitrary",), vmem_limit_bytes=VMEM_LIMIT_BYTES),
        name=name,
    )(x, norm_g, w_gu, w_down, final_g)


def _mixer_kernel(x_ref, ng_ref, wa_ref, wkg_ref, wglr_ref, wup_ref, wupt_ref, gb_ref, gbt_ref,
                  cw_ref, gn_ref, wb_ref, woc_ref, wog_ref, wo_ref, o_ref,
                  s_ref, carry_ref, pre_ref, og_ref):
    tm = x_ref.shape[0]
    i = pl.program_id(0)

    @pl.when(i % (SEQ // tm) == 0)
    def _():
        s_ref[...] = jnp.zeros_like(s_ref)
        carry_ref[...] = jnp.zeros_like(carry_ref)

    x = x_ref[...]
    h = _rmsnorm(x, ng_ref[...]).astype(BF16)

    cb = _dot(h, wa_ref[:, 0:D_MODEL])
    cc = _dot(h, wa_ref[:, D_MODEL:2 * D_MODEL])
    cu = _dot(h, wa_ref[:, 2 * D_MODEL:3 * D_MODEL])
    p = cc * cu
    row = lax.broadcasted_iota(jnp.int32, (tm, D_MODEL), 0)
    prev1 = carry_ref[7:8, :]
    prev2 = carry_ref[6:7, :]
    p1 = jnp.where(row == 0, prev1, pltpu.roll(p, 1, axis=0))
    p2 = jnp.where(row == 0, prev2, jnp.where(row == 1, prev1, pltpu.roll(p, 2, axis=0)))
    conv = p2 * cw_ref[0:1, :] + p1 * cw_ref[1:2, :] + p * cw_ref[2:3, :]
    pre_ref[...] = (cb * conv).astype(BF16)
    carry_ref[...] = p[tm - 8:tm, :]

    q = _dot(h, wa_ref[:, _OFF_Q:_OFF_Q + GLA_QK]) * (GLA_DK ** -0.5)
    v = _dot(h, wa_ref[:, _OFF_Q + GLA_QK:_OFF_Q + GLA_QK + GLA_V]).astype(BF16)
    glr = _dot(h, wglr_ref[...]).astype(BF16)
    log_a = _log_sigmoid(_dot(glr, wup_ref[...]) + gb_ref[...]) * (1.0 / GATE_TAU)

    kg_t = _dot_nt(wkg_ref[...], h)
    k_t = kg_t[0:GLA_QK, :]
    glr_t = kg_t[GLA_QK:KG_ROWS, :].astype(BF16)
    log_a_t = _log_sigmoid(_dot(wupt_ref[...], glr_t) + gbt_ref[...]) * (1.0 / GATE_TAU)

    ci = lax.broadcasted_iota(jnp.int32, (CUM_BLOCK, CUM_BLOCK), 0)
    cj = lax.broadcasted_iota(jnp.int32, (CUM_BLOCK, CUM_BLOCK), 1)
    same_chunk = (ci // CHUNK) == (cj // CHUNK)
    tril = jnp.where(same_chunk & (cj <= ci), 1.0, 0.0).astype(BF16)
    triu = jnp.where(same_chunk & (ci <= cj), 1.0, 0.0).astype(BF16)
    tril2 = jnp.concatenate([tril, tril], axis=1)
    triu2 = jnp.concatenate([triu, triu], axis=0)

    pi = lax.broadcasted_iota(jnp.int32, (PAIR, PAIR), 0)
    pj = lax.broadcasted_iota(jnp.int32, (PAIR, PAIR), 1)
    pair_mask = ((pi // CHUNK) == (pj // CHUNK)) & (pj <= pi)
    lane_lo = pj < CHUNK
    row_lo = pi < CHUNK

    for blk in range(tm // CUM_BLOCK):
        r0 = blk * CUM_BLOCK
        hi, lo = _split_hi_lo(log_a[r0:r0 + CUM_BLOCK, :])
        b_blk = _dot(tril2, jnp.concatenate([hi, lo], axis=0))
        hi_t, lo_t = _split_hi_lo(log_a_t[:, r0:r0 + CUM_BLOCK])
        bt_blk = _dot(jnp.concatenate([hi_t, lo_t], axis=1), triu2)

        for pr in range(CUM_BLOCK // PAIR):
            t0 = pr * PAIR
            v_pair = v[r0 + t0:r0 + t0 + PAIR, :]
            for hd in range(GLA_HEADS):
                k0 = hd * GLA_DK
                v0 = hd * GLA_DV
                b = b_blk[t0:t0 + PAIR, k0:k0 + GLA_DK]
                b_ref_row = jnp.where(row_lo, b[CHUNK // 2 - 1:CHUNK // 2, :],
                                      b[CHUNK + CHUNK // 2 - 1:CHUNK + CHUNK // 2, :])
                q_ref = q[r0 + t0:r0 + t0 + PAIR, k0:k0 + GLA_DK] * jnp.exp(b - b_ref_row)
                q_inter = (q_ref * jnp.exp(b_ref_row)).astype(BF16)

                bt = bt_blk[k0:k0 + GLA_DK, t0:t0 + PAIR]
                kt = k_t[k0:k0 + GLA_DK, r0 + t0:r0 + t0 + PAIR]
                b_ref_col = jnp.where(lane_lo, bt[:, CHUNK // 2 - 1:CHUNK // 2],
                                      bt[:, CHUNK + CHUNK // 2 - 1:CHUNK + CHUNK // 2])
                b_last0 = bt[:, CHUNK - 1:CHUNK]
                b_last1 = bt[:, PAIR - 1:PAIR]
                b_last_col = jnp.where(lane_lo, b_last0, b_last1)
                k_ref_t = (kt * jnp.exp(b_ref_col - bt)).astype(BF16)
                k_state_t = kt * jnp.exp(b_last_col - bt)
                k_state0 = jnp.where(lane_lo, k_state_t, 0.0).astype(BF16)
                k_state1 = jnp.where(lane_lo, 0.0, k_state_t).astype(BF16)

                scores = jnp.where(pair_mask, _dot(q_ref.astype(BF16), k_ref_t), 0.0).astype(BF16)
                vh = v_pair[:, v0:v0 + GLA_DV]
                stacked = _dot(jnp.concatenate([scores, k_state0, k_state1], axis=0), vh)
                o_intra = stacked[0:PAIR]
                kv0 = stacked[PAIR:PAIR + GLA_DK]
                kv1 = stacked[PAIR + GLA_DK:PAIR + 2 * GLA_DK]

                s0 = s_ref[hd]
                s1 = s0 * jnp.exp(b_last0) + kv0
                s2 = s1 * jnp.exp(b_last1) + kv1
                s_ref[hd] = s2
                o_inter = jnp.concatenate(
                    [_dot(q_inter[0:CHUNK], s0.astype(BF16)),
                     _dot(q_inter[CHUNK:PAIR], s1.astype(BF16))], axis=0)
                o = o_intra + o_inter
                o = o * lax.rsqrt(jnp.mean(o * o, axis=-1, keepdims=True) + EPS)
                og_ref[r0 + t0:r0 + t0 + PAIR, v0:v0 + GLA_DV] = o * gn_ref[:, v0:v0 + GLA_DV]

    r = _dot(h, wb_ref[:, 0:D_MODEL])
    og = (og_ref[...] * (r * jax.nn.sigmoid(r))).astype(BF16)
    y_gla = _dot(og, wog_ref[...])
    y_conv = _dot(pre_ref[...], woc_ref[...])
    ga = _dot(h, wb_ref[:, D_MODEL:2 * D_MODEL])
    gb = _dot(h, wb_ref[:, 2 * D_MODEL:3 * D_MODEL])
    merged = (jax.nn.sigmoid(ga) * y_conv + jax.nn.sigmoid(gb) * y_gla).astype(BF16)
    o_ref[...] = x + _dot(merged, wo_ref[...])


def _mixer(x, norm_g, w_in, conv_w, gate_w_up, gate_b, gla_norm_g, w_out_conv, w_out_gla, w_o):
    t = x.shape[0]
    tm = TM_MIX
    w_a = jnp.concatenate([w_in[:, _OFF_CONV:_OFF_K], w_in[:, _OFF_V:_OFF_GLR]], axis=1).astype(BF16)
    w_glr = w_in[:, _OFF_GLR:_OFF_R]
    w_kg_t = jnp.concatenate(
        [w_in[:, _OFF_K:_OFF_V].T, w_glr.T, jnp.zeros((LANES - GATE_RANK, D_MODEL), F32)],
        axis=0).astype(BF16)
    w_glr_p = jnp.pad(w_glr, ((0, 0), (0, LANES - GATE_RANK))).astype(BF16)
    w_up = jnp.pad(gate_w_up, ((0, LANES - GATE_RANK), (0, 0))).astype(BF16)
    w_up_t = w_up.T
    w_b = w_in[:, _OFF_R:_OFF_END].astype(BF16)
    tile = pl.BlockSpec((tm, D_MODEL), lambda i: (i, 0))
    return pl.pallas_call(
        _mixer_kernel,
        grid=(t // tm,),
        in_specs=[tile, _resident((1, D_MODEL)), _resident(w_a.shape), _resident(w_kg_t.shape),
                  _resident(w_glr_p.shape), _resident(w_up.shape), _resident(w_up_t.shape),
                  _resident((1, GLA_QK)), _resident((GLA_QK, 1)), _resident((CONV_K, D_MODEL)),
                  _resident((1, GLA_V)), _resident(w_b.shape), _resident((D_MODEL, D_MODEL)),
                  _resident((GLA_V, D_MODEL)), _resident((D_MODEL, D_MODEL))],
        out_specs=tile,
        out_shape=jax.ShapeDtypeStruct((t, D_MODEL), F32),
        scratch_shapes=[pltpu.VMEM((GLA_HEADS, GLA_DK, GLA_DV), F32),
                        pltpu.VMEM((8, D_MODEL), F32),
                        pltpu.VMEM((tm, D_MODEL), BF16),
                        pltpu.VMEM((tm, GLA_V), F32)],
        compiler_params=pltpu.CompilerParams(
            dimension_semantics=("arbitrary",), vmem_limit_bytes=VMEM_LIMIT_BYTES),
        name="mixer",
    )(x, norm_g, w_a, w_kg_t, w_glr_p, w_up, w_up_t, gate_b.reshape(1, GLA_QK),
      gate_b.reshape(GLA_QK, 1), conv_w, gla_norm_g.reshape(1, GLA_V), w_b,
      w_out_conv.astype(BF16), w_out_gla.astype(BF16), w_o.astype(BF16))


def kernel(x, norm_ffn1_g, ffn1_w_gu, ffn1_w_down, norm_mix_g, w_in, conv_w, gate_w_up, gate_b,
           gla_norm_g, w_out_conv, w_out_gla, w_o, norm_ffn2_g, ffn2_w_gu, ffn2_w_down, final_norm_g):
    bsz, seq, d = x.shape
    assert (seq, d) == (SEQ, D_MODEL) and norm_ffn1_g.shape[0] == 1
    xt = x.reshape(bsz * seq, d)
    fg = final_norm_g.reshape(1, d)
    xt = _ffn(xt, norm_ffn1_g, ffn1_w_gu[0].astype(BF16), ffn1_w_down[0].astype(BF16), fg,
              final_norm=False, name="ffn1")
    xt = _mixer(xt, norm_mix_g, w_in[0], conv_w[0], gate_w_up[0], gate_b[0], gla_norm_g[0],
                w_out_conv[0], w_out_gla[0], w_o[0])
    xt = _ffn(xt, norm_ffn2_g, ffn2_w_gu[0].astype(BF16), ffn2_w_down[0].astype(BF16), fg,
              final_norm=True, name="ffn2")
    return xt.reshape(bsz, seq, d)
```

```python
import functools

import jax
import jax.numpy as jnp
from jax import lax
from jax.experimental import pallas as pl
from jax.experimental.pallas import tpu as pltpu

D_MODEL = 1024
SEQ = 4096
D_FF = 2816
EPS = 1e-6
FFN_RES_WEIGHT = 0.5
CONV_K = 3
GLA_HEADS = 4
GLA_DK = 128
GLA_DV = 256
GLA_QK = GLA_HEADS * GLA_DK
GLA_V = GLA_HEADS * GLA_DV
GATE_RANK = 16
GATE_TAU = 16.0
CHUNK = 64
PAIR = 2 * CHUNK
LANES = 128
CUM_BLOCK = 256

_OFF_CONV = 0
_OFF_Q = 3 * D_MODEL
_OFF_K = _OFF_Q + GLA_QK
_OFF_V = _OFF_K + GLA_QK
_OFF_GLR = _OFF_V + GLA_V
_OFF_R = _OFF_GLR + GATE_RANK
_OFF_END = _OFF_R + 3 * D_MODEL

TM_FFN = 512
TM_MIX = 256
VMEM_LIMIT_BYTES = 56 * 1024 * 1024

F32 = jnp.float32
BF16 = jnp.bfloat16


def _rmsnorm(x, g):
    return x * lax.rsqrt(jnp.mean(x * x, axis=-1, keepdims=True) + EPS) * g


def _dot(a, b):
    return jnp.dot(a, b, preferred_element_type=F32)


def _dot_nt(a, b):
    return lax.dot_general(a, b, (((1,), (1,)), ((), ())), preferred_element_type=F32)


def _log_sigmoid(x):
    return jnp.minimum(x, 0.0) - jnp.log1p(jnp.exp(-jnp.abs(x)))


def _split_hi_lo(x):
    hi = x.astype(BF16)
    lo = (x - hi.astype(F32)).astype(BF16)
    return hi, lo


def _ffn_kernel(x_ref, g_ref, wgu_ref, wd_ref, fg_ref, o_ref, a_ref, *, final_norm):
    x = x_ref[...]
    h = _rmsnorm(x, g_ref[...]).astype(BF16)
    nb = 256
    for c in range(D_FF // nb):
        g = _dot(h, wgu_ref[:, c * nb:(c + 1) * nb])
        u = _dot(h, wgu_ref[:, D_FF + c * nb:D_FF + (c + 1) * nb])
        a_ref[:, c * nb:(c + 1) * nb] = (g * jax.nn.sigmoid(g) * u).astype(BF16)
    y = x + FFN_RES_WEIGHT * _dot(a_ref[...], wd_ref[...])
    if final_norm:
        y = _rmsnorm(y, fg_ref[...])
    o_ref[...] = y


def _resident(shape):
    return pl.BlockSpec(shape, lambda i: (0,) * len(shape), pipeline_mode=pl.Buffered(1))


def _ffn(x, norm_g, w_gu, w_down, final_g, *, final_norm, name):
    t = x.shape[0]
    tile = pl.BlockSpec((TM_FFN, D_MODEL), lambda i: (i, 0))
    return pl.pallas_call(
        functools.partial(_ffn_kernel, final_norm=final_norm),
        grid=(t // TM_FFN,),
        in_specs=[tile, _resident((1, D_MODEL)), _resident((D_MODEL, 2 * D_FF)),
                  _resident((D_FF, D_MODEL)), _resident((1, D_MODEL))],
        out_specs=tile,
        out_shape=jax.ShapeDtypeStruct((t, D_MODEL), F32),
        scratch_shapes=[pltpu.VMEM((TM_FFN, D_FF), BF16)],
        compiler_params=pltpu.CompilerParams(
            dimension_semantics=("arbitrary",), vmem_limit_bytes=VMEM_LIMIT_BYTES),
        name=name,
    )(x, norm_g, w_gu, w_down, final_g)


N_DENSE = 6 * D_MODEL
DENSE_BLOCK = 256
N_GLA_IN = GLA_QK + GLA_V + LANES


def _mixer_kernel(x_ref, ng_ref, wgla_ref, wkt_ref, wup_ref, gb_ref, cw_ref, gn_ref, wd_ref,
                  woc_ref, wog_ref, wo_ref, o_ref, s_ref, carry_ref, z_ref, og_ref):
    tm = x_ref.shape[0]
    i = pl.program_id(0)

    @pl.when(i % (SEQ // tm) == 0)
    def _():
        s_ref[...] = jnp.zeros_like(s_ref)
        carry_ref[...] = jnp.zeros_like(carry_ref)

    x = x_ref[...]
    h = _rmsnorm(x, ng_ref[...]).astype(BF16)

    q = _dot(h, wgla_ref[:, 0:GLA_QK]) * (GLA_DK ** -0.5)
    v = _dot(h, wgla_ref[:, GLA_QK:GLA_QK + GLA_V]).astype(BF16)
    glr = _dot(h, wgla_ref[:, GLA_QK + GLA_V:N_GLA_IN]).astype(BF16)
    log_a = _log_sigmoid(_dot(glr, wup_ref[...]) + gb_ref[...]) * (1.0 / GATE_TAU)
    k_t = _dot_nt(wkt_ref[...], h)

    ci = lax.broadcasted_iota(jnp.int32, (CUM_BLOCK, CUM_BLOCK), 0)
    cj = lax.broadcasted_iota(jnp.int32, (CUM_BLOCK, CUM_BLOCK), 1)
    tril = jnp.where(((ci // CHUNK) == (cj // CHUNK)) & (cj <= ci), 1.0, 0.0).astype(BF16)
    tril2 = jnp.concatenate([tril, tril], axis=1)

    pi = lax.broadcasted_iota(jnp.int32, (PAIR, PAIR), 0)
    pj = lax.broadcasted_iota(jnp.int32, (PAIR, PAIR), 1)
    pair_mask = ((pi // CHUNK) == (pj // CHUNK)) & (pj <= pi)
    lane_lo = pj < CHUNK
    row_lo = pi < CHUNK

    n_steps = (tm // PAIR) * GLA_HEADS
    n_dense = N_DENSE // DENSE_BLOCK
    step = 0

    def dense_blocks(lo, hi):
        for c in range(lo, hi):
            cols = slice(c * DENSE_BLOCK, (c + 1) * DENSE_BLOCK)
            z_ref[:, cols] = _dot(h, wd_ref[:, cols])

    for blk in range(tm // CUM_BLOCK):
        r0 = blk * CUM_BLOCK
        hi, lo = _split_hi_lo(log_a[r0:r0 + CUM_BLOCK, :])
        b_blk = _dot(tril2, jnp.concatenate([hi, lo], axis=0))
        bt_blk = b_blk.T

        for pr in range(CUM_BLOCK // PAIR):
            t0 = pr * PAIR
            v_pair = v[r0 + t0:r0 + t0 + PAIR, :]
            for hd in range(GLA_HEADS):
                dense_blocks(step * n_dense // n_steps, (step + 1) * n_dense // n_steps)
                step += 1
                k0 = hd * GLA_DK
                v0 = hd * GLA_DV
                b = b_blk[t0:t0 + PAIR, k0:k0 + GLA_DK]
                b_ref_row = jnp.where(row_lo, b[CHUNK // 2 - 1:CHUNK // 2, :],
                                      b[CHUNK + CHUNK // 2 - 1:CHUNK + CHUNK // 2, :])
                q_ref = q[r0 + t0:r0 + t0 + PAIR, k0:k0 + GLA_DK] * jnp.exp(b - b_ref_row)
                q_inter = (q_ref * jnp.exp(b_ref_row)).astype(BF16)

                bt = bt_blk[k0:k0 + GLA_DK, t0:t0 + PAIR]
                kt = k_t[k0:k0 + GLA_DK, r0 + t0:r0 + t0 + PAIR]
                b_ref_col = jnp.where(lane_lo, bt[:, CHUNK // 2 - 1:CHUNK // 2],
                                      bt[:, CHUNK + CHUNK // 2 - 1:CHUNK + CHUNK // 2])
                b_last0 = bt[:, CHUNK - 1:CHUNK]
                b_last1 = bt[:, PAIR - 1:PAIR]
                b_last_col = jnp.where(lane_lo, b_last0, b_last1)
                k_ref_t = (kt * jnp.exp(b_ref_col - bt)).astype(BF16)
                k_state_t = kt * jnp.exp(b_last_col - bt)
                k_state0 = jnp.where(lane_lo, k_state_t, 0.0).astype(BF16)
                k_state1 = jnp.where(lane_lo, 0.0, k_state_t).astype(BF16)

                scores = jnp.where(pair_mask, _dot(q_ref.astype(BF16), k_ref_t), 0.0).astype(BF16)
                vh = v_pair[:, v0:v0 + GLA_DV]
                stacked = _dot(jnp.concatenate([scores, k_state0, k_state1], axis=0), vh)
                o_intra = stacked[0:PAIR]
                kv0 = stacked[PAIR:PAIR + GLA_DK]
                kv1 = stacked[PAIR + GLA_DK:PAIR + 2 * GLA_DK]

                s0 = s_ref[hd]
                s1 = s0 * jnp.exp(b_last0) + kv0
                s2 = s1 * jnp.exp(b_last1) + kv1
                s_ref[hd] = s2
                o_inter = jnp.concatenate(
                    [_dot(q_inter[0:CHUNK], s0.astype(BF16)),
                     _dot(q_inter[CHUNK:PAIR], s1.astype(BF16))], axis=0)
                o = o_intra + o_inter
                o = o * lax.rsqrt(jnp.mean(o * o, axis=-1, keepdims=True) + EPS)
                og_ref[r0 + t0:r0 + t0 + PAIR, v0:v0 + GLA_DV] = o * gn_ref[:, v0:v0 + GLA_DV]

    p = z_ref[:, D_MODEL:2 * D_MODEL] * z_ref[:, 2 * D_MODEL:3 * D_MODEL]
    row = lax.broadcasted_iota(jnp.int32, (tm, D_MODEL), 0)
    prev1 = carry_ref[7:8, :]
    prev2 = carry_ref[6:7, :]
    p1 = jnp.where(row == 0, prev1, pltpu.roll(p, 1, axis=0))
    p2 = jnp.where(row == 0, prev2, jnp.where(row == 1, prev1, pltpu.roll(p, 2, axis=0)))
    conv = p2 * cw_ref[0:1, :] + p1 * cw_ref[1:2, :] + p * cw_ref[2:3, :]
    pre = (z_ref[:, 0:D_MODEL] * conv).astype(BF16)
    carry_ref[...] = p[tm - 8:tm, :]

    r = z_ref[:, 3 * D_MODEL:4 * D_MODEL]
    og = (og_ref[...] * (r * jax.nn.sigmoid(r))).astype(BF16)
    y_gla = _dot(og, wog_ref[...])
    y_conv = _dot(pre, woc_ref[...])
    ga = z_ref[:, 4 * D_MODEL:5 * D_MODEL]
    gb = z_ref[:, 5 * D_MODEL:6 * D_MODEL]
    merged = (jax.nn.sigmoid(ga) * y_conv + jax.nn.sigmoid(gb) * y_gla).astype(BF16)
    o_ref[...] = x + _dot(merged, wo_ref[...])


def _mixer(x, norm_g, w_in, conv_w, gate_w_up, gate_b, gla_norm_g, w_out_conv, w_out_gla, w_o):
    t = x.shape[0]
    tm = TM_MIX
    w_gla = jnp.concatenate(
        [w_in[:, _OFF_Q:_OFF_K], w_in[:, _OFF_V:_OFF_GLR],
         jnp.pad(w_in[:, _OFF_GLR:_OFF_R], ((0, 0), (0, LANES - GATE_RANK)))], axis=1).astype(BF16)
    w_k_t = w_in[:, _OFF_K:_OFF_V].T.astype(BF16)
    w_up = jnp.pad(gate_w_up, ((0, LANES - GATE_RANK), (0, 0))).astype(BF16)
    w_dense = jnp.concatenate([w_in[:, _OFF_CONV:_OFF_Q], w_in[:, _OFF_R:_OFF_END]], axis=1).astype(BF16)
    tile = pl.BlockSpec((tm, D_MODEL), lambda i: (i, 0))
    return pl.pallas_call(
        _mixer_kernel,
        grid=(t // tm,),
        in_specs=[tile, _resident((1, D_MODEL)), _resident(w_gla.shape), _resident(w_k_t.shape),
                  _resident(w_up.shape), _resident((1, GLA_QK)), _resident((CONV_K, D_MODEL)),
                  _resident((1, GLA_V)), _resident(w_dense.shape), _resident((D_MODEL, D_MODEL)),
                  _resident((GLA_V, D_MODEL)), _resident((D_MODEL, D_MODEL))],
        out_specs=tile,
        out_shape=jax.ShapeDtypeStruct((t, D_MODEL), F32),
        scratch_shapes=[pltpu.VMEM((GLA_HEADS, GLA_DK, GLA_DV), F32),
                        pltpu.VMEM((8, D_MODEL), F32),
                        pltpu.VMEM((tm, N_DENSE), F32),
                        pltpu.VMEM((tm, GLA_V), F32)],
        compiler_params=pltpu.CompilerParams(
            dimension_semantics=("arbitrary",), vmem_limit_bytes=VMEM_LIMIT_BYTES),
        name="mixer",
    )(x, norm_g, w_gla, w_k_t, w_up, gate_b.reshape(1, GLA_QK), conv_w,
      gla_norm_g.reshape(1, GLA_V), w_dense,
      w_out_conv.astype(BF16), w_out_gla.astype(BF16), w_o.astype(BF16))


def kernel(x, norm_ffn1_g, ffn1_w_gu, ffn1_w_down, norm_mix_g, w_in, conv_w, gate_w_up, gate_b,
           gla_norm_g, w_out_conv, w_out_gla, w_o, norm_ffn2_g, ffn2_w_gu, ffn2_w_down, final_norm_g):
    bsz, seq, d = x.shape
    assert (seq, d) == (SEQ, D_MODEL) and norm_ffn1_g.shape[0] == 1
    xt = x.reshape(bsz * seq, d)
    fg = final_norm_g.reshape(1, d)
    xt = _ffn(xt, norm_ffn1_g, ffn1_w_gu[0].astype(BF16), ffn1_w_down[0].astype(BF16), fg,
              final_norm=False, name="ffn1")
    xt = _mixer(xt, norm_mix_g, w_in[0], conv_w[0], gate_w_up[0], gate_b[0], gla_norm_g[0],
                w_out_conv[0], w_out_gla[0], w_o[0])
    xt = _ffn(xt, norm_ffn2_g, ffn2_w_gu[0].astype(BF16), ffn2_w_down[0].astype(BF16), fg,
              final_norm=True, name="ffn2")
    return xt.reshape(bsz, seq, d)
```

```python
import functools

import jax
import jax.numpy as jnp
from jax import lax
from jax.experimental import pallas as pl
from jax.experimental.pallas import tpu as pltpu

D_MODEL = 1024
SEQ = 4096
D_FF = 2816
EPS = 1e-6
FFN_RES_WEIGHT = 0.5
CONV_K = 3
GLA_HEADS = 4
GLA_DK = 128
GLA_DV = 256
GLA_QK = GLA_HEADS * GLA_DK
GLA_V = GLA_HEADS * GLA_DV
GATE_RANK = 16
GATE_TAU = 16.0
CHUNK = 64
PAIR = 2 * CHUNK
LANES = 128
CUM_BLOCK = 256

_OFF_CONV = 0
_OFF_Q = 3 * D_MODEL
_OFF_K = _OFF_Q + GLA_QK
_OFF_V = _OFF_K + GLA_QK
_OFF_GLR = _OFF_V + GLA_V
_OFF_R = _OFF_GLR + GATE_RANK
_OFF_END = _OFF_R + 3 * D_MODEL

TM_FFN = 512
TM_MIX = 512
VMEM_LIMIT_BYTES = 56 * 1024 * 1024

F32 = jnp.float32
BF16 = jnp.bfloat16


def _rmsnorm(x, g):
    return x * lax.rsqrt(jnp.mean(x * x, axis=-1, keepdims=True) + EPS) * g


def _dot(a, b):
    return jnp.dot(a, b, preferred_element_type=F32)


def _dot_nt(a, b):
    return lax.dot_general(a, b, (((1,), (1,)), ((), ())), preferred_element_type=F32)


def _log_sigmoid(x):
    return jnp.minimum(x, 0.0) - jnp.log1p(jnp.exp(-jnp.abs(x)))


def _split_hi_lo(x):
    hi = x.astype(BF16)
    lo = (x - hi.astype(F32)).astype(BF16)
    return hi, lo


def _ffn_kernel(x_ref, g_ref, wgu_ref, wd_ref, fg_ref, o_ref, a_ref, *, final_norm):
    x = x_ref[...]
    h = _rmsnorm(x, g_ref[...]).astype(BF16)
    nb = 256
    for c in range(D_FF // nb):
        g = _dot(h, wgu_ref[:, c * nb:(c + 1) * nb])
        u = _dot(h, wgu_ref[:, D_FF + c * nb:D_FF + (c + 1) * nb])
        a_ref[:, c * nb:(c + 1) * nb] = (g * jax.nn.sigmoid(g) * u).astype(BF16)
    y = x + FFN_RES_WEIGHT * _dot(a_ref[...], wd_ref[...])
    if final_norm:
        y = _rmsnorm(y, fg_ref[...])
    o_ref[...] = y


def _resident(shape):
    return pl.BlockSpec(shape, lambda i: (0,) * len(shape), pipeline_mode=pl.Buffered(1))


def _ffn(x, norm_g, w_gu, w_down, final_g, *, final_norm, name):
    t = x.shape[0]
    tile = pl.BlockSpec((TM_FFN, D_MODEL), lambda i: (i, 0))
    return pl.pallas_call(
        functools.partial(_ffn_kernel, final_norm=final_norm),
        grid=(t // TM_FFN,),
        in_specs=[tile, _resident((1, D_MODEL)), _resident((D_MODEL, 2 * D_FF)),
                  _resident((D_FF, D_MODEL)), _resident((1, D_MODEL))],
        out_specs=tile,
        out_shape=jax.ShapeDtypeStruct((t, D_MODEL), F32),
        scratch_shapes=[pltpu.VMEM((TM_FFN, D_FF), BF16)],
        compiler_params=pltpu.CompilerParams(
            dimension_semantics=("arbitrary",), vmem_limit_bytes=VMEM_LIMIT_BYTES),
        name=name,
    )(x, norm_g, w_gu, w_down, final_g)


N_DENSE = 6 * D_MODEL
DENSE_BLOCK = 256
N_GLA_IN = GLA_QK + GLA_V + LANES


def _mixer_kernel(x_ref, ng_ref, wgla_ref, wkt_ref, wup_ref, gb_ref, cw_ref, gn_ref, wd_ref,
                  woc_ref, wog_ref, wo_ref, o_ref, s_ref, carry_ref, z_ref, og_ref):
    tm = x_ref.shape[0]
    i = pl.program_id(0)

    @pl.when(i % (SEQ // tm) == 0)
    def _():
        s_ref[...] = jnp.zeros_like(s_ref)
        carry_ref[...] = jnp.zeros_like(carry_ref)

    x = x_ref[...]
    h = _rmsnorm(x, ng_ref[...]).astype(BF16)

    q = _dot(h, wgla_ref[:, 0:GLA_QK]) * (GLA_DK ** -0.5)
    v = _dot(h, wgla_ref[:, GLA_QK:GLA_QK + GLA_V]).astype(BF16)
    glr = _dot(h, wgla_ref[:, GLA_QK + GLA_V:N_GLA_IN]).astype(BF16)
    log_a = _log_sigmoid(_dot(glr, wup_ref[...]) + gb_ref[...]) * (1.0 / GATE_TAU)
    k_t = _dot_nt(wkt_ref[...], h)

    ci = lax.broadcasted_iota(jnp.int32, (CUM_BLOCK, CUM_BLOCK), 0)
    cj = lax.broadcasted_iota(jnp.int32, (CUM_BLOCK, CUM_BLOCK), 1)
    tril = jnp.where(((ci // CHUNK) == (cj // CHUNK)) & (cj <= ci), 1.0, 0.0).astype(BF16)
    tril2 = jnp.concatenate([tril, tril], axis=1)

    pi = lax.broadcasted_iota(jnp.int32, (PAIR, PAIR), 0)
    pj = lax.broadcasted_iota(jnp.int32, (PAIR, PAIR), 1)
    pair_mask = ((pi // CHUNK) == (pj // CHUNK)) & (pj <= pi)
    lane_lo = pj < CHUNK
    row_lo = pi < CHUNK

    n_stages = 4
    n_steps = (tm // PAIR) * n_stages
    n_dense = N_DENSE // DENSE_BLOCK
    step = 0

    def dense_step():
        nonlocal step
        for c in range(step * n_dense // n_steps, (step + 1) * n_dense // n_steps):
            cols = slice(c * DENSE_BLOCK, (c + 1) * DENSE_BLOCK)
            z_ref[:, cols] = _dot(h, wd_ref[:, cols])
        step += 1

    heads = range(GLA_HEADS)
    for blk in range(tm // CUM_BLOCK):
        r0 = blk * CUM_BLOCK
        hi, lo = _split_hi_lo(log_a[r0:r0 + CUM_BLOCK, :])
        b_blk = _dot(tril2, jnp.concatenate([hi, lo], axis=0))
        bt_blk = b_blk.T

        for pr in range(CUM_BLOCK // PAIR):
            t0 = pr * PAIR
            rows = slice(r0 + t0, r0 + t0 + PAIR)
            q_inter, k_state, scores, b_last = [], [], [], []
            dense_step()
            for hd in heads:
                dk = slice(hd * GLA_DK, (hd + 1) * GLA_DK)
                b = b_blk[t0:t0 + PAIR, dk]
                b_ref_row = jnp.where(row_lo, b[CHUNK // 2 - 1:CHUNK // 2, :],
                                      b[CHUNK + CHUNK // 2 - 1:CHUNK + CHUNK // 2, :])
                q_ref = q[rows, dk] * jnp.exp(b - b_ref_row)
                q_inter.append((q_ref * jnp.exp(b_ref_row)).astype(BF16))

                bt = bt_blk[dk, t0:t0 + PAIR]
                kt = k_t[dk, rows]
                b_ref_col = jnp.where(lane_lo, bt[:, CHUNK // 2 - 1:CHUNK // 2],
                                      bt[:, CHUNK + CHUNK // 2 - 1:CHUNK + CHUNK // 2])
                b_last0 = bt[:, CHUNK - 1:CHUNK]
                b_last1 = bt[:, PAIR - 1:PAIR]
                b_last.append((b_last0, b_last1))
                k_ref_t = (kt * jnp.exp(b_ref_col - bt)).astype(BF16)
                k_state_t = kt * jnp.exp(jnp.where(lane_lo, b_last0, b_last1) - bt)
                k_state.append((jnp.where(lane_lo, k_state_t, 0.0).astype(BF16),
                                jnp.where(lane_lo, 0.0, k_state_t).astype(BF16)))
                scores.append(_dot(q_ref.astype(BF16), k_ref_t))

            dense_step()
            stacked = []
            for hd in heads:
                sc = jnp.where(pair_mask, scores[hd], 0.0).astype(BF16)
                lhs = jnp.concatenate([sc, k_state[hd][0], k_state[hd][1]], axis=0)
                stacked.append(_dot(lhs, v[rows, hd * GLA_DV:(hd + 1) * GLA_DV]))

            dense_step()
            o_inter = []
            for hd in heads:
                kv0 = stacked[hd][PAIR:PAIR + GLA_DK]
                kv1 = stacked[hd][PAIR + GLA_DK:PAIR + 2 * GLA_DK]
                s0 = s_ref[hd]
                s1 = s0 * jnp.exp(b_last[hd][0]) + kv0
                s_ref[hd] = s1 * jnp.exp(b_last[hd][1]) + kv1
                o_inter.append((_dot(q_inter[hd][0:CHUNK], s0.astype(BF16)),
                                _dot(q_inter[hd][CHUNK:PAIR], s1.astype(BF16))))

            dense_step()
            for hd in heads:
                dv = slice(hd * GLA_DV, (hd + 1) * GLA_DV)
                o = stacked[hd][0:PAIR] + jnp.concatenate(o_inter[hd], axis=0)
                o = o * lax.rsqrt(jnp.mean(o * o, axis=-1, keepdims=True) + EPS)
                og_ref[rows, dv] = o * gn_ref[:, dv]
    assert step == n_steps

    p = z_ref[:, D_MODEL:2 * D_MODEL] * z_ref[:, 2 * D_MODEL:3 * D_MODEL]
    row = lax.broadcasted_iota(jnp.int32, (tm, D_MODEL), 0)
    prev1 = carry_ref[7:8, :]
    prev2 = carry_ref[6:7, :]
    p1 = jnp.where(row == 0, prev1, pltpu.roll(p, 1, axis=0))
    p2 = jnp.where(row == 0, prev2, jnp.where(row == 1, prev1, pltpu.roll(p, 2, axis=0)))
    conv = p2 * cw_ref[0:1, :] + p1 * cw_ref[1:2, :] + p * cw_ref[2:3, :]
    pre = (z_ref[:, 0:D_MODEL] * conv).astype(BF16)
    carry_ref[...] = p[tm - 8:tm, :]

    r = z_ref[:, 3 * D_MODEL:4 * D_MODEL]
    og = (og_ref[...] * (r * jax.nn.sigmoid(r))).astype(BF16)
    y_gla = _dot(og, wog_ref[...])
    y_conv = _dot(pre, woc_ref[...])
    ga = z_ref[:, 4 * D_MODEL:5 * D_MODEL]
    gb = z_ref[:, 5 * D_MODEL:6 * D_MODEL]
    merged = (jax.nn.sigmoid(ga) * y_conv + jax.nn.sigmoid(gb) * y_gla).astype(BF16)
    o_ref[...] = x + _dot(merged, wo_ref[...])


def _mixer(x, norm_g, w_in, conv_w, gate_w_up, gate_b, gla_norm_g, w_out_conv, w_out_gla, w_o):
    t = x.shape[0]
    tm = TM_MIX
    w_gla = jnp.concatenate(
        [w_in[:, _OFF_Q:_OFF_K], w_in[:, _OFF_V:_OFF_GLR],
         jnp.pad(w_in[:, _OFF_GLR:_OFF_R], ((0, 0), (0, LANES - GATE_RANK)))], axis=1).astype(BF16)
    w_k_t = w_in[:, _OFF_K:_OFF_V].T.astype(BF16)
    w_up = jnp.pad(gate_w_up, ((0, LANES - GATE_RANK), (0, 0))).astype(BF16)
    w_dense = jnp.concatenate([w_in[:, _OFF_CONV:_OFF_Q], w_in[:, _OFF_R:_OFF_END]], axis=1).astype(BF16)
    tile = pl.BlockSpec((tm, D_MODEL), lambda i: (i, 0))
    return pl.pallas_call(
        _mixer_kernel,
        grid=(t // tm,),
        in_specs=[tile, _resident((1, D_MODEL)), _resident(w_gla.shape), _resident(w_k_t.shape),
                  _resident(w_up.shape), _resident((1, GLA_QK)), _resident((CONV_K, D_MODEL)),
                  _resident((1, GLA_V)), _resident(w_dense.shape), _resident((D_MODEL, D_MODEL)),
                  _resident((GLA_V, D_MODEL)), _resident((D_MODEL, D_MODEL))],
        out_specs=tile,
        out_shape=jax.ShapeDtypeStruct((t, D_MODEL), F32),
        scratch_shapes=[pltpu.VMEM((GLA_HEADS, GLA_DK, GLA_DV), F32),
                        pltpu.VMEM((8, D_MODEL), F32),
                        pltpu.VMEM((tm, N_DENSE), F32),
                        pltpu.VMEM((tm, GLA_V), F32)],
        compiler_params=pltpu.CompilerParams(
            dimension_semantics=("arbitrary",), vmem_limit_bytes=VMEM_LIMIT_BYTES),
        name="mixer",
    )(x, norm_g, w_gla, w_k_t, w_up, gate_b.reshape(1, GLA_QK), conv_w,
      gla_norm_g.reshape(1, GLA_V), w_dense,
      w_out_conv.astype(BF16), w_out_gla.astype(BF16), w_o.astype(BF16))


def kernel(x, norm_ffn1_g, ffn1_w_gu, ffn1_w_down, norm_mix_g, w_in, conv_w, gate_w_up, gate_b,
           gla_norm_g, w_out_conv, w_out_gla, w_o, norm_ffn2_g, ffn2_w_gu, ffn2_w_down, final_norm_g):
    bsz, seq, d = x.shape
    assert (seq, d) == (SEQ, D_MODEL) and norm_ffn1_g.shape[0] == 1
    xt = x.reshape(bsz * seq, d)
    fg = final_norm_g.reshape(1, d)
    xt = _ffn(xt, norm_ffn1_g, ffn1_w_gu[0].astype(BF16), ffn1_w_down[0].astype(BF16), fg,
              final_norm=False, name="ffn1")
    xt = _mixer(xt, norm_mix_g, w_in[0], conv_w[0], gate_w_up[0], gate_b[0], gla_norm_g[0],
                w_out_conv[0], w_out_gla[0], w_o[0])
    xt = _ffn(xt, norm_ffn2_g, ffn2_w_gu[0].astype(BF16), ffn2_w_down[0].astype(BF16), fg,
              final_norm=True, name="ffn2")
    return xt.reshape(bsz, seq, d)
```

```python
import jax
import jax.numpy as jnp
from jax import lax
from jax.experimental import pallas as pl
from jax.experimental.pallas import tpu as pltpu

D_MODEL = 1024
SEQ = 4096
D_FF = 2816
EPS = 1e-6
FFN_RES_WEIGHT = 0.5
CONV_K = 3
GLA_HEADS = 4
GLA_DK = 128
GLA_DV = 256
GLA_QK = GLA_HEADS * GLA_DK
GLA_V = GLA_HEADS * GLA_DV
GATE_RANK = 16
GATE_TAU = 16.0
CHUNK = 64
PAIR = 2 * CHUNK
LANES = 128
CUM_BLOCK = 256

_OFF_CONV = 0
_OFF_Q = 3 * D_MODEL
_OFF_K = _OFF_Q + GLA_QK
_OFF_V = _OFF_K + GLA_QK
_OFF_GLR = _OFF_V + GLA_V
_OFF_R = _OFF_GLR + GATE_RANK
_OFF_END = _OFF_R + 3 * D_MODEL

TM_FFN = 512
TM_MIX = 512
VMEM_LIMIT_BYTES = 56 * 1024 * 1024

F32 = jnp.float32
BF16 = jnp.bfloat16


def _rmsnorm(x, g):
    return x * lax.rsqrt(jnp.mean(x * x, axis=-1, keepdims=True) + EPS) * g


def _dot(a, b):
    return jnp.dot(a, b, preferred_element_type=F32)


def _dot_nt(a, b):
    return lax.dot_general(a, b, (((1,), (1,)), ((), ())), preferred_element_type=F32)


def _log_sigmoid(x):
    return jnp.minimum(x, 0.0) - jnp.log1p(jnp.exp(-jnp.abs(x)))


def _split_hi_lo(x):
    hi = x.astype(BF16)
    lo = (x - hi.astype(F32)).astype(BF16)
    return hi, lo


def _ffn_body(x_ref, g_ref, wgu_ref, wd_ref, a_ref):
    x = x_ref[...]
    h = _rmsnorm(x, g_ref[...]).astype(BF16)
    nb = 256
    for c in range(D_FF // nb):
        g = _dot(h, wgu_ref[:, c * nb:(c + 1) * nb])
        u = _dot(h, wgu_ref[:, D_FF + c * nb:D_FF + (c + 1) * nb])
        a_ref[:, c * nb:(c + 1) * nb] = (g * jax.nn.sigmoid(g) * u).astype(BF16)
    return x + FFN_RES_WEIGHT * _dot(a_ref[...], wd_ref[...])


def _ffn1_kernel(x_ref, g_ref, wgu_ref, wd_ref, win_ref, woc_ref, wog_ref, wo_ref,
                 o_ref, wconv_o, wq_o, wk_o, wv_o, wglr_o, wgates_o, woc_o, wog_o, wo_o, a_ref):
    w = win_ref[...]
    wconv_o[...] = w[:, _OFF_CONV:_OFF_Q].astype(BF16)
    wq_o[...] = w[:, _OFF_Q:_OFF_K].astype(BF16)
    wk_o[...] = w[:, _OFF_K:_OFF_V].astype(BF16)
    wv_o[...] = w[:, _OFF_V:_OFF_GLR].astype(BF16)
    glr = w[:, _OFF_GLR:_OFF_GLR + LANES]
    lane = lax.broadcasted_iota(jnp.int32, glr.shape, 1)
    wglr_o[...] = jnp.where(lane < GATE_RANK, glr, 0.0).astype(BF16)
    wgates_o[...] = w[:, _OFF_R:_OFF_END].astype(BF16)
    woc_o[...] = woc_ref[...].astype(BF16)
    wog_o[...] = wog_ref[...].astype(BF16)
    wo_o[...] = wo_ref[...].astype(BF16)
    o_ref[...] = _ffn_body(x_ref, g_ref, wgu_ref, wd_ref, a_ref)


def _ffn2_kernel(x_ref, g_ref, wgu_ref, wd_ref, fg_ref, o_ref, a_ref):
    o_ref[...] = _rmsnorm(_ffn_body(x_ref, g_ref, wgu_ref, wd_ref, a_ref), fg_ref[...])


def _resident(shape):
    return pl.BlockSpec(shape, lambda i: (0,) * len(shape), pipeline_mode=pl.Buffered(1))


def _row_block(rows, cols):
    return pl.BlockSpec((rows, cols), lambda i: (i, 0))


_FFN_PARAMS = pltpu.CompilerParams(dimension_semantics=("arbitrary",),
                                   vmem_limit_bytes=VMEM_LIMIT_BYTES)


def _ffn1(x, norm_g, w_gu, w_down, w_in, w_out_conv, w_out_gla, w_o):
    t = x.shape[0]
    steps = t // TM_FFN
    rows = D_MODEL // steps
    tile = _row_block(TM_FFN, D_MODEL)
    piece_cols = (3 * D_MODEL, GLA_QK, GLA_QK, GLA_V, LANES, 3 * D_MODEL, D_MODEL, D_MODEL, D_MODEL)
    return pl.pallas_call(
        _ffn1_kernel,
        grid=(steps,),
        in_specs=[tile, _resident((1, D_MODEL)), _resident((D_MODEL, 2 * D_FF)),
                  _resident((D_FF, D_MODEL)), _row_block(rows, w_in.shape[1]),
                  _row_block(rows, D_MODEL), _row_block(rows, D_MODEL), _row_block(rows, D_MODEL)],
        out_specs=[tile] + [_row_block(rows, c) for c in piece_cols],
        out_shape=[jax.ShapeDtypeStruct((t, D_MODEL), F32)]
        + [jax.ShapeDtypeStruct((D_MODEL, c), BF16) for c in piece_cols],
        scratch_shapes=[pltpu.VMEM((TM_FFN, D_FF), BF16)],
        compiler_params=_FFN_PARAMS,
        name="ffn1",
    )(x, norm_g, w_gu, w_down, w_in, w_out_conv, w_out_gla, w_o)


def _ffn2(x, norm_g, w_gu, w_down, final_g):
    t = x.shape[0]
    tile = _row_block(TM_FFN, D_MODEL)
    return pl.pallas_call(
        _ffn2_kernel,
        grid=(t // TM_FFN,),
        in_specs=[tile, _resident((1, D_MODEL)), _resident((D_MODEL, 2 * D_FF)),
                  _resident((D_FF, D_MODEL)), _resident((1, D_MODEL))],
        out_specs=tile,
        out_shape=jax.ShapeDtypeStruct((t, D_MODEL), F32),
        scratch_shapes=[pltpu.VMEM((TM_FFN, D_FF), BF16)],
        compiler_params=_FFN_PARAMS,
        name="ffn2",
    )(x, norm_g, w_gu, w_down, final_g)


N_DENSE = 6 * D_MODEL
DENSE_BLOCK = 256
WD2_ROWS = 128


def _mixer_kernel(x_ref, ng_ref, wq_ref, wv_ref, wglr_ref, wkt_ref, wup_ref, gb_ref, cw_ref, gn_ref,
                  wconv_ref, wgates_ref, woc_ref, wog_ref, wo_ref, wgu2_ref, wd2_ref,
                  o_ref, wgu2_o, wd2_o, s_ref, carry_ref, z_ref, og_ref):
    tm = x_ref.shape[0]
    i = pl.program_id(0)

    wgu2_o[...] = wgu2_ref[...].astype(BF16)

    @pl.when(i < D_FF // WD2_ROWS)
    def _():
        wd2_o[...] = wd2_ref[...].astype(BF16)

    @pl.when(i % (SEQ // tm) == 0)
    def _():
        s_ref[...] = jnp.zeros_like(s_ref)
        carry_ref[...] = jnp.zeros_like(carry_ref)

    x = x_ref[...]
    h = _rmsnorm(x, ng_ref[...]).astype(BF16)

    q = _dot(h, wq_ref[...]) * (GLA_DK ** -0.5)
    v = _dot(h, wv_ref[...]).astype(BF16)
    glr = _dot(h, wglr_ref[...]).astype(BF16)
    log_a = _log_sigmoid(_dot(glr, wup_ref[...]) + gb_ref[...]) * (1.0 / GATE_TAU)
    k_t = _dot_nt(wkt_ref[...], h)

    ci = lax.broadcasted_iota(jnp.int32, (CUM_BLOCK, CUM_BLOCK), 0)
    cj = lax.broadcasted_iota(jnp.int32, (CUM_BLOCK, CUM_BLOCK), 1)
    tril = jnp.where(((ci // CHUNK) == (cj // CHUNK)) & (cj <= ci), 1.0, 0.0).astype(BF16)
    tril2 = jnp.concatenate([tril, tril], axis=1)

    pi = lax.broadcasted_iota(jnp.int32, (PAIR, PAIR), 0)
    pj = lax.broadcasted_iota(jnp.int32, (PAIR, PAIR), 1)
    pair_mask = ((pi // CHUNK) == (pj // CHUNK)) & (pj <= pi)
    lane_lo = pj < CHUNK
    row_lo = pi < CHUNK

    n_stages = 4
    n_steps = (tm // PAIR) * n_stages
    n_dense = N_DENSE // DENSE_BLOCK
    step = 0

    def dense_step():
        nonlocal step
        half = n_dense // 2
        for c in range(step * n_dense // n_steps, (step + 1) * n_dense // n_steps):
            w_ref, cw = (wconv_ref, c) if c < half else (wgates_ref, c - half)
            z_ref[:, c * DENSE_BLOCK:(c + 1) * DENSE_BLOCK] = _dot(
                h, w_ref[:, cw * DENSE_BLOCK:(cw + 1) * DENSE_BLOCK])
        step += 1

    heads = range(GLA_HEADS)
    for blk in range(tm // CUM_BLOCK):
        r0 = blk * CUM_BLOCK
        hi, lo = _split_hi_lo(log_a[r0:r0 + CUM_BLOCK, :])
        b_blk = _dot(tril2, jnp.concatenate([hi, lo], axis=0))
        bt_blk = b_blk.T

        for pr in range(CUM_BLOCK // PAIR):
            t0 = pr * PAIR
            rows = slice(r0 + t0, r0 + t0 + PAIR)
            q_inter, k_state, scores, b_last = [], [], [], []
            dense_step()
            for hd in heads:
                dk = slice(hd * GLA_DK, (hd + 1) * GLA_DK)
                b = b_blk[t0:t0 + PAIR, dk]
                b_ref_row = jnp.where(row_lo, b[CHUNK // 2 - 1:CHUNK // 2, :],
                                      b[CHUNK + CHUNK // 2 - 1:CHUNK + CHUNK // 2, :])
                q_ref = q[rows, dk] * jnp.exp(b - b_ref_row)
                q_inter.append((q_ref * jnp.exp(b_ref_row)).astype(BF16))

                bt = bt_blk[dk, t0:t0 + PAIR]
                kt = k_t[dk, rows]
                b_ref_col = jnp.where(lane_lo, bt[:, CHUNK // 2 - 1:CHUNK // 2],
                                      bt[:, CHUNK + CHUNK // 2 - 1:CHUNK + CHUNK // 2])
                b_last0 = bt[:, CHUNK - 1:CHUNK]
                b_last1 = bt[:, PAIR - 1:PAIR]
                b_last.append((b_last0, b_last1))
                k_ref_t = (kt * jnp.exp(b_ref_col - bt)).astype(BF16)
                k_state_t = kt * jnp.exp(jnp.where(lane_lo, b_last0, b_last1) - bt)
                k_state.append((jnp.where(lane_lo, k_state_t, 0.0).astype(BF16),
                                jnp.where(lane_lo, 0.0, k_state_t).astype(BF16)))
                scores.append(_dot(q_ref.astype(BF16), k_ref_t))

            dense_step()
            stacked = []
            for hd in heads:
                sc = jnp.where(pair_mask, scores[hd], 0.0).astype(BF16)
                lhs = jnp.concatenate([sc, k_state[hd][0], k_state[hd][1]], axis=0)
                stacked.append(_dot(lhs, v[rows, hd * GLA_DV:(hd + 1) * GLA_DV]))

            dense_step()
            o_inter = []
            for hd in heads:
                kv0 = stacked[hd][PAIR:PAIR + GLA_DK]
                kv1 = stacked[hd][PAIR + GLA_DK:PAIR + 2 * GLA_DK]
                s0 = s_ref[hd]
                s1 = s0 * jnp.exp(b_last[hd][0]) + kv0
                s_ref[hd] = s1 * jnp.exp(b_last[hd][1]) + kv1
                o_inter.append((_dot(q_inter[hd][0:CHUNK], s0.astype(BF16)),
                                _dot(q_inter[hd][CHUNK:PAIR], s1.astype(BF16))))

            dense_step()
            for hd in heads:
                dv = slice(hd * GLA_DV, (hd + 1) * GLA_DV)
                o = stacked[hd][0:PAIR] + jnp.concatenate(o_inter[hd], axis=0)
                o = o * lax.rsqrt(jnp.mean(o * o, axis=-1, keepdims=True) + EPS)
                og_ref[rows, dv] = o * gn_ref[:, dv]
    assert step == n_steps

    p = z_ref[:, D_MODEL:2 * D_MODEL] * z_ref[:, 2 * D_MODEL:3 * D_MODEL]
    row = lax.broadcasted_iota(jnp.int32, (tm, D_MODEL), 0)
    prev1 = carry_ref[7:8, :]
    prev2 = carry_ref[6:7, :]
    p1 = jnp.where(row == 0, prev1, pltpu.roll(p, 1, axis=0))
    p2 = jnp.where(row == 0, prev2, jnp.where(row == 1, prev1, pltpu.roll(p, 2, axis=0)))
    conv = p2 * cw_ref[0:1, :] + p1 * cw_ref[1:2, :] + p * cw_ref[2:3, :]
    pre = (z_ref[:, 0:D_MODEL] * conv).astype(BF16)
    carry_ref[...] = p[tm - 8:tm, :]

    r = z_ref[:, 3 * D_MODEL:4 * D_MODEL]
    og = (og_ref[...] * (r * jax.nn.sigmoid(r))).astype(BF16)
    y_gla = _dot(og, wog_ref[...])
    y_conv = _dot(pre, woc_ref[...])
    ga = z_ref[:, 4 * D_MODEL:5 * D_MODEL]
    gb = z_ref[:, 5 * D_MODEL:6 * D_MODEL]
    merged = (jax.nn.sigmoid(ga) * y_conv + jax.nn.sigmoid(gb) * y_gla).astype(BF16)
    o_ref[...] = x + _dot(merged, wo_ref[...])


def _mixer(x, norm_g, w_conv, w_q, w_k, w_v, w_glr, w_gates, w_oc, w_og, w_o, conv_w, gate_w_up,
           gate_b, gla_norm_g, w_gu2, w_down2):
    t = x.shape[0]
    tm = TM_MIX
    steps = t // tm
    assert steps * WD2_ROWS >= D_FF and D_FF % WD2_ROWS == 0
    w_up = jnp.pad(gate_w_up, ((0, LANES - GATE_RANK), (0, 0))).astype(BF16)
    tile = _row_block(tm, D_MODEL)
    gu_rows = D_MODEL // steps
    wd2_block = pl.BlockSpec((WD2_ROWS, D_MODEL),
                             lambda i: (jnp.minimum(i, D_FF // WD2_ROWS - 1), 0))
    return pl.pallas_call(
        _mixer_kernel,
        grid=(steps,),
        in_specs=[tile, _resident((1, D_MODEL)), _resident(w_q.shape), _resident(w_v.shape),
                  _resident(w_glr.shape), _resident((GLA_QK, D_MODEL)), _resident(w_up.shape),
                  _resident((1, GLA_QK)), _resident((CONV_K, D_MODEL)), _resident((1, GLA_V)),
                  _resident(w_conv.shape), _resident(w_gates.shape), _resident(w_oc.shape),
                  _resident(w_og.shape), _resident(w_o.shape),
                  _row_block(gu_rows, 2 * D_FF), wd2_block],
        out_specs=[tile, _row_block(gu_rows, 2 * D_FF), wd2_block],
        out_shape=[jax.ShapeDtypeStruct((t, D_MODEL), F32),
                   jax.ShapeDtypeStruct((D_MODEL, 2 * D_FF), BF16),
                   jax.ShapeDtypeStruct((D_FF, D_MODEL), BF16)],
        scratch_shapes=[pltpu.VMEM((GLA_HEADS, GLA_DK, GLA_DV), F32),
                        pltpu.VMEM((8, D_MODEL), F32),
                        pltpu.VMEM((tm, N_DENSE), F32),
                        pltpu.VMEM((tm, GLA_V), F32)],
        compiler_params=pltpu.CompilerParams(
            dimension_semantics=("arbitrary",), vmem_limit_bytes=VMEM_LIMIT_BYTES),
        name="mixer",
    )(x, norm_g, w_q, w_v, w_glr, w_k.T, w_up, gate_b.reshape(1, GLA_QK), conv_w,
      gla_norm_g.reshape(1, GLA_V), w_conv, w_gates, w_oc, w_og, w_o, w_gu2, w_down2)


def kernel(x, norm_ffn1_g, ffn1_w_gu, ffn1_w_down, norm_mix_g, w_in, conv_w, gate_w_up, gate_b,
           gla_norm_g, w_out_conv, w_out_gla, w_o, norm_ffn2_g, ffn2_w_gu, ffn2_w_down, final_norm_g):
    bsz, seq, d = x.shape
    assert (seq, d) == (SEQ, D_MODEL) and norm_ffn1_g.shape[0] == 1
    xt = x.reshape(bsz * seq, d)
    xt, w_conv, w_q, w_k, w_v, w_glr, w_gates, w_oc, w_og, w_ob = _ffn1(
        xt, norm_ffn1_g, ffn1_w_gu[0].astype(BF16), ffn1_w_down[0].astype(BF16),
        w_in[0], w_out_conv[0], w_out_gla[0], w_o[0])
    xt, w_gu2, w_down2 = _mixer(
        xt, norm_mix_g, w_conv, w_q, w_k, w_v, w_glr, w_gates, w_oc, w_og, w_ob,
        conv_w[0], gate_w_up[0], gate_b[0], gla_norm_g[0], ffn2_w_gu[0], ffn2_w_down[0])
    xt = _ffn2(xt, norm_ffn2_g, w_gu2, w_down2, final_norm_g.reshape(1, d))
    return xt.reshape(bsz, seq, d)
```

```python
import jax
import jax.numpy as jnp
from jax import lax
from jax.experimental import pallas as pl
from jax.experimental.pallas import tpu as pltpu

D_MODEL = 1024
SEQ = 4096
D_FF = 2816
EPS = 1e-6
FFN_RES_WEIGHT = 0.5
CONV_K = 3
GLA_HEADS = 4
GLA_DK = 128
GLA_DV = 256
GLA_QK = GLA_HEADS * GLA_DK
GLA_V = GLA_HEADS * GLA_DV
GATE_RANK = 16
GATE_TAU = 16.0
CHUNK = 64
PAIR = 2 * CHUNK
LANES = 128
CUM_BLOCK = 256

_OFF_CONV = 0
_OFF_Q = 3 * D_MODEL
_OFF_K = _OFF_Q + GLA_QK
_OFF_V = _OFF_K + GLA_QK
_OFF_GLR = _OFF_V + GLA_V
_OFF_R = _OFF_GLR + GATE_RANK
_OFF_END = _OFF_R + 3 * D_MODEL

TM_FFN = 512
TM_MIX = 512
VMEM_LIMIT_BYTES = 56 * 1024 * 1024

F32 = jnp.float32
BF16 = jnp.bfloat16


def _rmsnorm(x, g):
    return x * lax.rsqrt(jnp.mean(x * x, axis=-1, keepdims=True) + EPS) * g


def _dot(a, b):
    return jnp.dot(a, b, preferred_element_type=F32)


def _dot_nt(a, b):
    return lax.dot_general(a, b, (((1,), (1,)), ((), ())), preferred_element_type=F32)


def _log_sigmoid(x):
    return jnp.minimum(x, 0.0) - jnp.log1p(jnp.exp(-jnp.abs(x)))


def _split_hi_lo(x):
    hi = x.astype(BF16)
    lo = (x - hi.astype(F32)).astype(BF16)
    return hi, lo


def _ffn_body(x_ref, g_ref, wgu_ref, wd_ref, a_ref):
    x = x_ref[...]
    h = _rmsnorm(x, g_ref[...]).astype(BF16)
    nb = 256
    for c in range(D_FF // nb):
        g = _dot(h, wgu_ref[:, c * nb:(c + 1) * nb])
        u = _dot(h, wgu_ref[:, D_FF + c * nb:D_FF + (c + 1) * nb])
        a_ref[:, c * nb:(c + 1) * nb] = (g * jax.nn.sigmoid(g) * u).astype(BF16)
    return x + FFN_RES_WEIGHT * _dot(a_ref[...], wd_ref[...])


_WIN_PIECES = (
    (_OFF_CONV, 3 * D_MODEL // LANES, True),
    (_OFF_Q, GLA_QK // LANES, True),
    (_OFF_K, GLA_QK // LANES, False),
    (_OFF_V, GLA_V // LANES, True),
    (_OFF_GLR, 1, True),
    (_OFF_R, 3 * D_MODEL // LANES, True),
)
_N_PIECES = len(_WIN_PIECES)


def _ffn1_kernel(x_ref, g_ref, wgu_ref, wd_ref, *rest):
    win_refs = rest[:_N_PIECES]
    woc_ref, wog_ref, wo_ref, o_ref = rest[_N_PIECES:_N_PIECES + 4]
    piece_outs = rest[_N_PIECES + 4:2 * _N_PIECES + 4]
    woc_o, wog_o, wo_o, a_ref = rest[2 * _N_PIECES + 4:]
    i = pl.program_id(0)
    for (off, n_blocks, transposed), src, dst in zip(_WIN_PIECES, win_refs, piece_outs):
        @pl.when(i < n_blocks)
        def _(off=off, transposed=transposed, src=src, dst=dst):
            blk = src[...]
            if off == _OFF_GLR:
                row = lax.broadcasted_iota(jnp.int32, blk.shape, 0)
                blk = jnp.where(row < GATE_RANK, blk, 0.0)
            dst[...] = (blk.T if transposed else blk).astype(BF16)
    woc_o[...] = woc_ref[...].astype(BF16)
    wog_o[...] = wog_ref[...].astype(BF16)
    wo_o[...] = wo_ref[...].astype(BF16)
    o_ref[...] = _ffn_body(x_ref, g_ref, wgu_ref, wd_ref, a_ref)


def _ffn2_kernel(x_ref, g_ref, wgu_ref, wd_ref, fg_ref, o_ref, a_ref):
    o_ref[...] = _rmsnorm(_ffn_body(x_ref, g_ref, wgu_ref, wd_ref, a_ref), fg_ref[...])


def _resident(shape):
    return pl.BlockSpec(shape, lambda i: (0,) * len(shape), pipeline_mode=pl.Buffered(1))


def _row_block(rows, cols):
    return pl.BlockSpec((rows, cols), lambda i: (i, 0))


_FFN_PARAMS = pltpu.CompilerParams(dimension_semantics=("arbitrary",),
                                   vmem_limit_bytes=VMEM_LIMIT_BYTES)


def _ffn1(x, norm_g, w_gu, w_down, w_in_t, w_out_conv, w_out_gla, w_o):
    t = x.shape[0]
    steps = t // TM_FFN
    assert all(n <= steps for _, n, _ in _WIN_PIECES)
    rows = D_MODEL // steps
    tile = _row_block(TM_FFN, D_MODEL)
    piece_in, piece_out, piece_shape = [], [], []
    for off, n, transposed in _WIN_PIECES:
        piece_in.append(pl.BlockSpec(
            (pl.Element(LANES), pl.Element(D_MODEL)),
            lambda i, off=off, n=n: (pl.multiple_of(off + LANES * jnp.minimum(i, n - 1), 8), 0)))
        if transposed:
            piece_out.append(pl.BlockSpec((D_MODEL, LANES), lambda i, n=n: (0, jnp.minimum(i, n - 1))))
            piece_shape.append(jax.ShapeDtypeStruct((D_MODEL, n * LANES), BF16))
        else:
            piece_out.append(pl.BlockSpec((LANES, D_MODEL), lambda i, n=n: (jnp.minimum(i, n - 1), 0)))
            piece_shape.append(jax.ShapeDtypeStruct((n * LANES, D_MODEL), BF16))
    square = jax.ShapeDtypeStruct((D_MODEL, D_MODEL), BF16)
    return pl.pallas_call(
        _ffn1_kernel,
        grid=(steps,),
        in_specs=[tile, _resident((1, D_MODEL)), _resident((D_MODEL, 2 * D_FF)),
                  _resident((D_FF, D_MODEL))] + piece_in + [_row_block(rows, D_MODEL)] * 3,
        out_specs=[tile] + piece_out + [_row_block(rows, D_MODEL)] * 3,
        out_shape=[jax.ShapeDtypeStruct((t, D_MODEL), F32)] + piece_shape + [square] * 3,
        scratch_shapes=[pltpu.VMEM((TM_FFN, D_FF), BF16)],
        compiler_params=_FFN_PARAMS,
        name="ffn1",
    )(x, norm_g, w_gu, w_down, *([w_in_t] * _N_PIECES), w_out_conv, w_out_gla, w_o)


def _ffn2(x, norm_g, w_gu, w_down, final_g):
    t = x.shape[0]
    tile = _row_block(TM_FFN, D_MODEL)
    return pl.pallas_call(
        _ffn2_kernel,
        grid=(t // TM_FFN,),
        in_specs=[tile, _resident((1, D_MODEL)), _resident((D_MODEL, 2 * D_FF)),
                  _resident((D_FF, D_MODEL)), _resident((1, D_MODEL))],
        out_specs=tile,
        out_shape=jax.ShapeDtypeStruct((t, D_MODEL), F32),
        scratch_shapes=[pltpu.VMEM((TM_FFN, D_FF), BF16)],
        compiler_params=_FFN_PARAMS,
        name="ffn2",
    )(x, norm_g, w_gu, w_down, final_g)


N_DENSE = 6 * D_MODEL
DENSE_BLOCK = 256
WD2_ROWS = 128


def _mixer_kernel(x_ref, ng_ref, wq_ref, wv_ref, wglr_ref, wkt_ref, wup_ref, gb_ref, cw_ref, gn_ref,
                  wconv_ref, wgates_ref, woc_ref, wog_ref, wo_ref, wgu2_ref, wd2_ref,
                  o_ref, wgu2_o, wd2_o, s_ref, carry_ref, z_ref, og_ref):
    tm = x_ref.shape[0]
    i = pl.program_id(0)

    wgu2_o[...] = wgu2_ref[...].astype(BF16)

    @pl.when(i < D_FF // WD2_ROWS)
    def _():
        wd2_o[...] = wd2_ref[...].astype(BF16)

    @pl.when(i % (SEQ // tm) == 0)
    def _():
        s_ref[...] = jnp.zeros_like(s_ref)
        carry_ref[...] = jnp.zeros_like(carry_ref)

    x = x_ref[...]
    h = _rmsnorm(x, ng_ref[...]).astype(BF16)

    q = _dot(h, wq_ref[...]) * (GLA_DK ** -0.5)
    v = _dot(h, wv_ref[...]).astype(BF16)
    glr = _dot(h, wglr_ref[...]).astype(BF16)
    log_a = _log_sigmoid(_dot(glr, wup_ref[...]) + gb_ref[...]) * (1.0 / GATE_TAU)
    k_t = _dot_nt(wkt_ref[...], h)

    ci = lax.broadcasted_iota(jnp.int32, (CUM_BLOCK, CUM_BLOCK), 0)
    cj = lax.broadcasted_iota(jnp.int32, (CUM_BLOCK, CUM_BLOCK), 1)
    tril = jnp.where(((ci // CHUNK) == (cj // CHUNK)) & (cj <= ci), 1.0, 0.0).astype(BF16)
    tril2 = jnp.concatenate([tril, tril], axis=1)

    pi = lax.broadcasted_iota(jnp.int32, (PAIR, PAIR), 0)
    pj = lax.broadcasted_iota(jnp.int32, (PAIR, PAIR), 1)
    pair_mask = ((pi // CHUNK) == (pj // CHUNK)) & (pj <= pi)
    lane_lo = pj < CHUNK
    row_lo = pi < CHUNK

    n_stages = 4
    n_steps = (tm // PAIR) * n_stages
    n_dense = N_DENSE // DENSE_BLOCK
    step = 0

    def dense_step():
        nonlocal step
        half = n_dense // 2
        for c in range(step * n_dense // n_steps, (step + 1) * n_dense // n_steps):
            w_ref, cw = (wconv_ref, c) if c < half else (wgates_ref, c - half)
            z_ref[:, c * DENSE_BLOCK:(c + 1) * DENSE_BLOCK] = _dot(
                h, w_ref[:, cw * DENSE_BLOCK:(cw + 1) * DENSE_BLOCK])
        step += 1

    heads = range(GLA_HEADS)
    for blk in range(tm // CUM_BLOCK):
        r0 = blk * CUM_BLOCK
        hi, lo = _split_hi_lo(log_a[r0:r0 + CUM_BLOCK, :])
        b_blk = _dot(tril2, jnp.concatenate([hi, lo], axis=0))
        bt_blk = b_blk.T

        for pr in range(CUM_BLOCK // PAIR):
            t0 = pr * PAIR
            rows = slice(r0 + t0, r0 + t0 + PAIR)
            q_inter, k_state, scores, b_last = [], [], [], []
            dense_step()
            for hd in heads:
                dk = slice(hd * GLA_DK, (hd + 1) * GLA_DK)
                b = b_blk[t0:t0 + PAIR, dk]
                b_ref_row = jnp.where(row_lo, b[CHUNK // 2 - 1:CHUNK // 2, :],
                                      b[CHUNK + CHUNK // 2 - 1:CHUNK + CHUNK // 2, :])
                q_ref = q[rows, dk] * jnp.exp(b - b_ref_row)
                q_inter.append((q_ref * jnp.exp(b_ref_row)).astype(BF16))

                bt = bt_blk[dk, t0:t0 + PAIR]
                kt = k_t[dk, rows]
                b_ref_col = jnp.where(lane_lo, bt[:, CHUNK // 2 - 1:CHUNK // 2],
                                      bt[:, CHUNK + CHUNK // 2 - 1:CHUNK + CHUNK // 2])
                b_last0 = bt[:, CHUNK - 1:CHUNK]
                b_last1 = bt[:, PAIR - 1:PAIR]
                b_last.append((b_last0, b_last1))
                k_ref_t = (kt * jnp.exp(b_ref_col - bt)).astype(BF16)
                k_state_t = kt * jnp.exp(jnp.where(lane_lo, b_last0, b_last1) - bt)
                k_state.append((jnp.where(lane_lo, k_state_t, 0.0).astype(BF16),
                                jnp.where(lane_lo, 0.0, k_state_t).astype(BF16)))
                scores.append(_dot(q_ref.astype(BF16), k_ref_t))

            dense_step()
            stacked = []
            for hd in heads:
                sc = jnp.where(pair_mask, scores[hd], 0.0).astype(BF16)
                lhs = jnp.concatenate([sc, k_state[hd][0], k_state[hd][1]], axis=0)
                stacked.append(_dot(lhs, v[rows, hd * GLA_DV:(hd + 1) * GLA_DV]))

            dense_step()
            o_inter = []
            for hd in heads:
                kv0 = stacked[hd][PAIR:PAIR + GLA_DK]
                kv1 = stacked[hd][PAIR + GLA_DK:PAIR + 2 * GLA_DK]
                s0 = s_ref[hd]
                s1 = s0 * jnp.exp(b_last[hd][0]) + kv0
                s_ref[hd] = s1 * jnp.exp(b_last[hd][1]) + kv1
                o_inter.append((_dot(q_inter[hd][0:CHUNK], s0.astype(BF16)),
                                _dot(q_inter[hd][CHUNK:PAIR], s1.astype(BF16))))

            dense_step()
            for hd in heads:
                dv = slice(hd * GLA_DV, (hd + 1) * GLA_DV)
                o = stacked[hd][0:PAIR] + jnp.concatenate(o_inter[hd], axis=0)
                o = o * lax.rsqrt(jnp.mean(o * o, axis=-1, keepdims=True) + EPS)
                og_ref[rows, dv] = o * gn_ref[:, dv]
    assert step == n_steps

    p = z_ref[:, D_MODEL:2 * D_MODEL] * z_ref[:, 2 * D_MODEL:3 * D_MODEL]
    row = lax.broadcasted_iota(jnp.int32, (tm, D_MODEL), 0)
    prev1 = carry_ref[7:8, :]
    prev2 = carry_ref[6:7, :]
    p1 = jnp.where(row == 0, prev1, pltpu.roll(p, 1, axis=0))
    p2 = jnp.where(row == 0, prev2, jnp.where(row == 1, prev1, pltpu.roll(p, 2, axis=0)))
    conv = p2 * cw_ref[0:1, :] + p1 * cw_ref[1:2, :] + p * cw_ref[2:3, :]
    pre = (z_ref[:, 0:D_MODEL] * conv).astype(BF16)
    carry_ref[...] = p[tm - 8:tm, :]

    r = z_ref[:, 3 * D_MODEL:4 * D_MODEL]
    og = (og_ref[...] * (r * jax.nn.sigmoid(r))).astype(BF16)
    y_gla = _dot(og, wog_ref[...])
    y_conv = _dot(pre, woc_ref[...])
    ga = z_ref[:, 4 * D_MODEL:5 * D_MODEL]
    gb = z_ref[:, 5 * D_MODEL:6 * D_MODEL]
    merged = (jax.nn.sigmoid(ga) * y_conv + jax.nn.sigmoid(gb) * y_gla).astype(BF16)
    o_ref[...] = x + _dot(merged, wo_ref[...])


def _mixer(x, norm_g, w_conv, w_q, w_k_t, w_v, w_glr, w_gates, w_oc, w_og, w_o, conv_w, gate_w_up,
           gate_b, gla_norm_g, w_gu2, w_down2):
    t = x.shape[0]
    tm = TM_MIX
    steps = t // tm
    assert steps * WD2_ROWS >= D_FF and D_FF % WD2_ROWS == 0
    w_up = jnp.pad(gate_w_up, ((0, LANES - GATE_RANK), (0, 0))).astype(BF16)
    tile = _row_block(tm, D_MODEL)
    gu_rows = D_MODEL // steps
    wd2_block = pl.BlockSpec((WD2_ROWS, D_MODEL),
                             lambda i: (jnp.minimum(i, D_FF // WD2_ROWS - 1), 0))
    return pl.pallas_call(
        _mixer_kernel,
        grid=(steps,),
        in_specs=[tile, _resident((1, D_MODEL)), _resident(w_q.shape), _resident(w_v.shape),
                  _resident(w_glr.shape), _resident((GLA_QK, D_MODEL)), _resident(w_up.shape),
                  _resident((1, GLA_QK)), _resident((CONV_K, D_MODEL)), _resident((1, GLA_V)),
                  _resident(w_conv.shape), _resident(w_gates.shape), _resident(w_oc.shape),
                  _resident(w_og.shape), _resident(w_o.shape),
                  _row_block(gu_rows, 2 * D_FF), wd2_block],
        out_specs=[tile, _row_block(gu_rows, 2 * D_FF), wd2_block],
        out_shape=[jax.ShapeDtypeStruct((t, D_MODEL), F32),
                   jax.ShapeDtypeStruct((D_MODEL, 2 * D_FF), BF16),
                   jax.ShapeDtypeStruct((D_FF, D_MODEL), BF16)],
        scratch_shapes=[pltpu.VMEM((GLA_HEADS, GLA_DK, GLA_DV), F32),
                        pltpu.VMEM((8, D_MODEL), F32),
                        pltpu.VMEM((tm, N_DENSE), F32),
                        pltpu.VMEM((tm, GLA_V), F32)],
        compiler_params=pltpu.CompilerParams(
            dimension_semantics=("arbitrary",), vmem_limit_bytes=VMEM_LIMIT_BYTES),
        name="mixer",
    )(x, norm_g, w_q, w_v, w_glr, w_k_t, w_up, gate_b.reshape(1, GLA_QK), conv_w,
      gla_norm_g.reshape(1, GLA_V), w_conv, w_gates, w_oc, w_og, w_o, w_gu2, w_down2)


def kernel(x, norm_ffn1_g, ffn1_w_gu, ffn1_w_down, norm_mix_g, w_in, conv_w, gate_w_up, gate_b,
           gla_norm_g, w_out_conv, w_out_gla, w_o, norm_ffn2_g, ffn2_w_gu, ffn2_w_down, final_norm_g):
    bsz, seq, d = x.shape
    assert (seq, d) == (SEQ, D_MODEL) and norm_ffn1_g.shape[0] == 1
    xt = x.reshape(bsz * seq, d)
    xt, w_conv, w_q, w_k_t, w_v, w_glr, w_gates, w_oc, w_og, w_ob = _ffn1(
        xt, norm_ffn1_g, ffn1_w_gu[0].astype(BF16), ffn1_w_down[0].astype(BF16),
        w_in[0].T, w_out_conv[0], w_out_gla[0], w_o[0])
    xt, w_gu2, w_down2 = _mixer(
        xt, norm_mix_g, w_conv, w_q, w_k_t, w_v, w_glr, w_gates, w_oc, w_og, w_ob,
        conv_w[0], gate_w_up[0], gate_b[0], gla_norm_g[0], ffn2_w_gu[0], ffn2_w_down[0])
    xt = _ffn2(xt, norm_ffn2_g, w_gu2, w_down2, final_norm_g.reshape(1, d))
    return xt.reshape(bsz, seq, d)
```

```python
import jax
import jax.numpy as jnp
from jax import lax
from jax.experimental import pallas as pl
from jax.experimental.pallas import tpu as pltpu

D_MODEL = 1024
SEQ = 4096
D_FF = 2816
EPS = 1e-6
FFN_RES_WEIGHT = 0.5
CONV_K = 3
GLA_HEADS = 4
GLA_DK = 128
GLA_DV = 256
GLA_QK = GLA_HEADS * GLA_DK
GLA_V = GLA_HEADS * GLA_DV
GATE_RANK = 16
GATE_TAU = 16.0
CHUNK = 64
PAIR = 2 * CHUNK
LANES = 128
CUM_BLOCK = 256

_OFF_CONV = 0
_OFF_Q = 3 * D_MODEL
_OFF_K = _OFF_Q + GLA_QK
_OFF_V = _OFF_K + GLA_QK
_OFF_GLR = _OFF_V + GLA_V
_OFF_R = _OFF_GLR + GATE_RANK
_OFF_END = _OFF_R + 3 * D_MODEL

TM_FFN = 512
TM_MIX = 512
VMEM_LIMIT_BYTES = 56 * 1024 * 1024

F32 = jnp.float32
BF16 = jnp.bfloat16


def _rmsnorm(x, g):
    return x * lax.rsqrt(jnp.mean(x * x, axis=-1, keepdims=True) + EPS) * g


def _dot(a, b):
    return jnp.dot(a, b, preferred_element_type=F32)


def _dot_nt(a, b):
    return lax.dot_general(a, b, (((1,), (1,)), ((), ())), preferred_element_type=F32)


def _log_sigmoid(x):
    return jnp.minimum(x, 0.0) - jnp.log(1.0 + jnp.exp(-jnp.abs(x)))


def _split_hi_lo(x):
    hi = x.astype(BF16)
    lo = (x - hi.astype(F32)).astype(BF16)
    return hi, lo


def _ffn_body(x_ref, g_ref, wgu_ref, wd_ref, a_ref):
    x = x_ref[...]
    h = _rmsnorm(x, g_ref[...]).astype(BF16)
    nb = 256
    for c in range(D_FF // nb):
        g = _dot(h, wgu_ref[:, c * nb:(c + 1) * nb])
        u = _dot(h, wgu_ref[:, D_FF + c * nb:D_FF + (c + 1) * nb])
        a_ref[:, c * nb:(c + 1) * nb] = (g * jax.nn.sigmoid(g) * u).astype(BF16)
    return x + FFN_RES_WEIGHT * _dot(a_ref[...], wd_ref[...])


_WIN_PIECES = (
    (_OFF_CONV, 3 * D_MODEL // LANES, True),
    (_OFF_Q, GLA_QK // LANES, True),
    (_OFF_K, GLA_QK // LANES, False),
    (_OFF_V, GLA_V // LANES, True),
    (_OFF_GLR, 1, True),
    (_OFF_R, 3 * D_MODEL // LANES, True),
)
_N_PIECES = len(_WIN_PIECES)


def _ffn1_kernel(x_ref, g_ref, wgu_ref, wd_ref, *rest):
    win_refs = rest[:_N_PIECES]
    woc_ref, wog_ref, wo_ref, o_ref = rest[_N_PIECES:_N_PIECES + 4]
    piece_outs = rest[_N_PIECES + 4:2 * _N_PIECES + 4]
    woc_o, wog_o, wo_o, a_ref = rest[2 * _N_PIECES + 4:]
    i = pl.program_id(0)
    for (off, n_blocks, transposed), src, dst in zip(_WIN_PIECES, win_refs, piece_outs):
        @pl.when(i < n_blocks)
        def _(off=off, transposed=transposed, src=src, dst=dst):
            blk = src[...]
            if off == _OFF_GLR:
                row = lax.broadcasted_iota(jnp.int32, blk.shape, 0)
                blk = jnp.where(row < GATE_RANK, blk, 0.0)
            dst[...] = (blk.T if transposed else blk).astype(BF16)
    woc_o[...] = woc_ref[...].astype(BF16)
    wog_o[...] = wog_ref[...].astype(BF16)
    wo_o[...] = wo_ref[...].astype(BF16)
    o_ref[...] = _ffn_body(x_ref, g_ref, wgu_ref, wd_ref, a_ref)


def _ffn2_kernel(x_ref, g_ref, wgu_ref, wd_ref, fg_ref, o_ref, a_ref):
    o_ref[...] = _rmsnorm(_ffn_body(x_ref, g_ref, wgu_ref, wd_ref, a_ref), fg_ref[...])


def _resident(shape):
    return pl.BlockSpec(shape, lambda i: (0,) * len(shape), pipeline_mode=pl.Buffered(1))


def _row_block(rows, cols):
    return pl.BlockSpec((rows, cols), lambda i: (i, 0))


_FFN_PARAMS = pltpu.CompilerParams(dimension_semantics=("arbitrary",),
                                   vmem_limit_bytes=VMEM_LIMIT_BYTES)


def _ffn1(x, norm_g, w_gu, w_down, w_in_t, w_out_conv, w_out_gla, w_o):
    t = x.shape[0]
    steps = t // TM_FFN
    assert all(n <= steps for _, n, _ in _WIN_PIECES)
    rows = D_MODEL // steps
    tile = _row_block(TM_FFN, D_MODEL)
    piece_in, piece_out, piece_shape = [], [], []
    for off, n, transposed in _WIN_PIECES:
        piece_in.append(pl.BlockSpec(
            (pl.Element(LANES), pl.Element(D_MODEL)),
            lambda i, off=off, n=n: (pl.multiple_of(off + LANES * jnp.minimum(i, n - 1), 8), 0)))
        if transposed:
            piece_out.append(pl.BlockSpec((D_MODEL, LANES), lambda i, n=n: (0, jnp.minimum(i, n - 1))))
            piece_shape.append(jax.ShapeDtypeStruct((D_MODEL, n * LANES), BF16))
        else:
            piece_out.append(pl.BlockSpec((LANES, D_MODEL), lambda i, n=n: (jnp.minimum(i, n - 1), 0)))
            piece_shape.append(jax.ShapeDtypeStruct((n * LANES, D_MODEL), BF16))
    square = jax.ShapeDtypeStruct((D_MODEL, D_MODEL), BF16)
    return pl.pallas_call(
        _ffn1_kernel,
        grid=(steps,),
        in_specs=[tile, _resident((1, D_MODEL)), _resident((D_MODEL, 2 * D_FF)),
                  _resident((D_FF, D_MODEL))] + piece_in + [_row_block(rows, D_MODEL)] * 3,
        out_specs=[tile] + piece_out + [_row_block(rows, D_MODEL)] * 3,
        out_shape=[jax.ShapeDtypeStruct((t, D_MODEL), F32)] + piece_shape + [square] * 3,
        scratch_shapes=[pltpu.VMEM((TM_FFN, D_FF), BF16)],
        compiler_params=_FFN_PARAMS,
        name="ffn1",
    )(x, norm_g, w_gu, w_down, *([w_in_t] * _N_PIECES), w_out_conv, w_out_gla, w_o)


def _ffn2(x, norm_g, w_gu, w_down, final_g):
    t = x.shape[0]
    tile = _row_block(TM_FFN, D_MODEL)
    return pl.pallas_call(
        _ffn2_kernel,
        grid=(t // TM_FFN,),
        in_specs=[tile, _resident((1, D_MODEL)), _resident((D_MODEL, 2 * D_FF)),
                  _resident((D_FF, D_MODEL)), _resident((1, D_MODEL))],
        out_specs=tile,
        out_shape=jax.ShapeDtypeStruct((t, D_MODEL), F32),
        scratch_shapes=[pltpu.VMEM((TM_FFN, D_FF), BF16)],
        compiler_params=_FFN_PARAMS,
        name="ffn2",
    )(x, norm_g, w_gu, w_down, final_g)


N_DENSE = 6 * D_MODEL
DENSE_BLOCK = 256
WD2_ROWS = 128
P_HEAD = 8


def _mixer_kernel(x_ref, ng_ref, wq_ref, wv_ref, wglr_ref, wkt_ref, wup_ref, gb_ref, cw_ref, gn_ref,
                  wconv_ref, wgates_ref, woc_ref, wog_ref, wo_ref, wgu2_ref, wd2_ref,
                  o_ref, wgu2_o, wd2_o, s_ref, carry_ref, z_ref, og_ref):
    tm = x_ref.shape[0]
    i = pl.program_id(0)

    wgu2_o[...] = wgu2_ref[...].astype(BF16)

    @pl.when(i < D_FF // WD2_ROWS)
    def _():
        wd2_o[...] = wd2_ref[...].astype(BF16)

    @pl.when(i % (SEQ // tm) == 0)
    def _():
        s_ref[...] = jnp.zeros_like(s_ref)
        carry_ref[...] = jnp.zeros_like(carry_ref)

    x = x_ref[...]
    h = _rmsnorm(x, ng_ref[...]).astype(BF16)

    glr = _dot(h, wglr_ref[...]).astype(BF16)

    ci = lax.broadcasted_iota(jnp.int32, (CUM_BLOCK, CUM_BLOCK), 0)
    cj = lax.broadcasted_iota(jnp.int32, (CUM_BLOCK, CUM_BLOCK), 1)
    tril = jnp.where(((ci // CHUNK) == (cj // CHUNK)) & (cj <= ci), 1.0, 0.0).astype(BF16)
    tril2 = jnp.concatenate([tril, tril], axis=1)

    pi = lax.broadcasted_iota(jnp.int32, (PAIR, PAIR), 0)
    pj = lax.broadcasted_iota(jnp.int32, (PAIR, PAIR), 1)
    pair_mask = ((pi // CHUNK) == (pj // CHUNK)) & (pj <= pi)
    lane_lo = pj < CHUNK
    row_lo = pi < CHUNK

    def decay_block(blk):
        rows = slice(blk * CUM_BLOCK, (blk + 1) * CUM_BLOCK)
        log_a = _log_sigmoid(_dot(glr[rows], wup_ref[...]) + gb_ref[...]) * (1.0 / GATE_TAU)
        hi, lo = _split_hi_lo(log_a)
        b_blk = _dot(tril2, jnp.concatenate([hi, lo], axis=0))
        return b_blk, b_blk.T

    n_stages = 4
    n_lead = 4
    n_steps = (tm // PAIR) * n_stages
    n_dense = N_DENSE // DENSE_BLOCK
    issued = 0

    def dense_blocks(count):
        nonlocal issued
        half = n_dense // 2
        for c in range(issued, issued + count):
            w_ref, cw = (wconv_ref, c) if c < half else (wgates_ref, c - half)
            z_ref[:, c * DENSE_BLOCK:(c + 1) * DENSE_BLOCK] = _dot(
                h, w_ref[:, cw * DENSE_BLOCK:(cw + 1) * DENSE_BLOCK])
        issued += count

    step = 0

    def dense_step():
        nonlocal step
        step += 1
        dense_blocks(n_lead + step * (n_dense - n_lead) // n_steps - issued)

    q = _dot(h, wq_ref[...]) * (GLA_DK ** -0.5)
    decay = {0: decay_block(0)}
    v = _dot(h, wv_ref[...]).astype(BF16)
    k_t = _dot_nt(wkt_ref[...], h)
    dense_blocks(n_lead)
    heads = range(GLA_HEADS)
    n_blk = tm // CUM_BLOCK
    for blk in range(n_blk):
        r0 = blk * CUM_BLOCK
        b_blk, bt_blk = decay.pop(blk)

        for pr in range(CUM_BLOCK // PAIR):
            t0 = pr * PAIR
            rows = slice(r0 + t0, r0 + t0 + PAIR)
            q_inter, k_state, scores, b_last = [], [], [], []
            dense_step()
            for hd in heads:
                dk = slice(hd * GLA_DK, (hd + 1) * GLA_DK)
                b = b_blk[t0:t0 + PAIR, dk]
                b_ref0 = b[CHUNK // 2 - 1:CHUNK // 2, :]
                b_ref1 = b[CHUNK + CHUNK // 2 - 1:CHUNK + CHUNK // 2, :]
                q_ref = q[rows, dk] * jnp.exp(b - jnp.where(row_lo, b_ref0, b_ref1))
                q_inter.append(
                    (q_ref * jnp.where(row_lo, jnp.exp(b_ref0), jnp.exp(b_ref1))).astype(BF16))

                bt = bt_blk[dk, t0:t0 + PAIR]
                kt = k_t[dk, rows]
                b_ref_col = jnp.where(lane_lo, bt[:, CHUNK // 2 - 1:CHUNK // 2],
                                      bt[:, CHUNK + CHUNK // 2 - 1:CHUNK + CHUNK // 2])
                b_last0 = bt[:, CHUNK - 1:CHUNK]
                b_last1 = bt[:, PAIR - 1:PAIR]
                b_last.append((b_last0, b_last1))
                k_ref_t = (kt * jnp.exp(b_ref_col - bt)).astype(BF16)
                k_state_t = kt * jnp.exp(jnp.where(lane_lo, b_last0, b_last1) - bt)
                k_state.append((jnp.where(lane_lo, k_state_t, 0.0).astype(BF16),
                                jnp.where(lane_lo, 0.0, k_state_t).astype(BF16)))
                scores.append(_dot(q_ref.astype(BF16), k_ref_t))

            dense_step()
            stacked = []
            for hd in heads:
                sc = jnp.where(pair_mask, scores[hd], 0.0).astype(BF16)
                lhs = jnp.concatenate([sc, k_state[hd][0], k_state[hd][1]], axis=0)
                stacked.append(_dot(lhs, v[rows, hd * GLA_DV:(hd + 1) * GLA_DV]))

            if pr == 0 and blk + 1 < n_blk:
                decay[blk + 1] = decay_block(blk + 1)

            dense_step()
            o_inter = []
            for hd in heads:
                kv0 = stacked[hd][PAIR:PAIR + GLA_DK]
                kv1 = stacked[hd][PAIR + GLA_DK:PAIR + 2 * GLA_DK]
                s0 = s_ref[hd]
                s1 = s0 * jnp.exp(b_last[hd][0]) + kv0
                s_ref[hd] = s1 * jnp.exp(b_last[hd][1]) + kv1
                o_inter.append((_dot(q_inter[hd][0:CHUNK], s0.astype(BF16)),
                                _dot(q_inter[hd][CHUNK:PAIR], s1.astype(BF16))))

            dense_step()
            for hd in heads:
                dv = slice(hd * GLA_DV, (hd + 1) * GLA_DV)
                o = stacked[hd][0:PAIR] + jnp.concatenate(o_inter[hd], axis=0)
                o = o * lax.rsqrt(jnp.mean(o * o, axis=-1, keepdims=True) + EPS)
                og_ref[rows, dv] = o * gn_ref[:, dv]
    assert step == n_steps and issued == n_dense

    p = z_ref[:, D_MODEL:2 * D_MODEL] * z_ref[:, 2 * D_MODEL:3 * D_MODEL]
    row = lax.broadcasted_iota(jnp.int32, (tm, D_MODEL), 0)
    prev1 = carry_ref[P_HEAD - 1:P_HEAD, :]
    prev2 = carry_ref[P_HEAD - 2:P_HEAD - 1, :]
    p1 = jnp.where(row == 0, prev1, pltpu.roll(p, 1, axis=0))
    p2 = jnp.where(row == 0, prev2, jnp.where(row == 1, prev1, pltpu.roll(p, 2, axis=0)))
    conv = p2 * cw_ref[0:1, :] + p1 * cw_ref[1:2, :] + p * cw_ref[2:3, :]
    pre = (z_ref[:, 0:D_MODEL] * conv).astype(BF16)
    carry_ref[...] = p[tm - P_HEAD:tm, :]

    r = z_ref[:, 3 * D_MODEL:4 * D_MODEL]
    og = (og_ref[...] * (r * jax.nn.sigmoid(r))).astype(BF16)
    y_gla = _dot(og, wog_ref[...])
    y_conv = _dot(pre, woc_ref[...])
    ga = z_ref[:, 4 * D_MODEL:5 * D_MODEL]
    gb = z_ref[:, 5 * D_MODEL:6 * D_MODEL]
    merged = (jax.nn.sigmoid(ga) * y_conv + jax.nn.sigmoid(gb) * y_gla).astype(BF16)
    o_ref[...] = x + _dot(merged, wo_ref[...])


def _mixer(x, norm_g, w_conv, w_q, w_k_t, w_v, w_glr, w_gates, w_oc, w_og, w_o, conv_w, gate_w_up,
           gate_b, gla_norm_g, w_gu2, w_down2):
    t = x.shape[0]
    tm = TM_MIX
    steps = t // tm
    assert steps * WD2_ROWS >= D_FF and D_FF % WD2_ROWS == 0
    w_up = jnp.pad(gate_w_up, ((0, LANES - GATE_RANK), (0, 0))).astype(BF16)
    tile = _row_block(tm, D_MODEL)
    gu_rows = D_MODEL // steps
    wd2_block = pl.BlockSpec((WD2_ROWS, D_MODEL),
                             lambda i: (jnp.minimum(i, D_FF // WD2_ROWS - 1), 0))
    return pl.pallas_call(
        _mixer_kernel,
        grid=(steps,),
        in_specs=[tile, _resident((1, D_MODEL)), _resident(w_q.shape), _resident(w_v.shape),
                  _resident(w_glr.shape), _resident((GLA_QK, D_MODEL)), _resident(w_up.shape),
                  _resident((1, GLA_QK)), _resident((CONV_K, D_MODEL)), _resident((1, GLA_V)),
                  _resident(w_conv.shape), _resident(w_gates.shape), _resident(w_oc.shape),
                  _resident(w_og.shape), _resident(w_o.shape),
                  _row_block(gu_rows, 2 * D_FF), wd2_block],
        out_specs=[tile, _row_block(gu_rows, 2 * D_FF), wd2_block],
        out_shape=[jax.ShapeDtypeStruct((t, D_MODEL), F32),
                   jax.ShapeDtypeStruct((D_MODEL, 2 * D_FF), BF16),
                   jax.ShapeDtypeStruct((D_FF, D_MODEL), BF16)],
        scratch_shapes=[pltpu.VMEM((GLA_HEADS, GLA_DK, GLA_DV), F32),
                        pltpu.VMEM((P_HEAD, D_MODEL), F32),
                        pltpu.VMEM((tm, N_DENSE), F32),
                        pltpu.VMEM((tm, GLA_V), F32)],
        compiler_params=pltpu.CompilerParams(
            dimension_semantics=("arbitrary",), vmem_limit_bytes=VMEM_LIMIT_BYTES),
        name="mixer",
    )(x, norm_g, w_q, w_v, w_glr, w_k_t, w_up, gate_b.reshape(1, GLA_QK), conv_w,
      gla_norm_g.reshape(1, GLA_V), w_conv, w_gates, w_oc, w_og, w_o, w_gu2, w_down2)


def kernel(x, norm_ffn1_g, ffn1_w_gu, ffn1_w_down, norm_mix_g, w_in, conv_w, gate_w_up, gate_b,
           gla_norm_g, w_out_conv, w_out_gla, w_o, norm_ffn2_g, ffn2_w_gu, ffn2_w_down, final_norm_g):
    bsz, seq, d = x.shape
    assert (seq, d) == (SEQ, D_MODEL) and norm_ffn1_g.shape[0] == 1
    xt = x.reshape(bsz * seq, d)
    xt, w_conv, w_q, w_k_t, w_v, w_glr, w_gates, w_oc, w_og, w_ob = _ffn1(
        xt, norm_ffn1_g, ffn1_w_gu[0].astype(BF16), ffn1_w_down[0].astype(BF16),
        w_in[0].T, w_out_conv[0], w_out_gla[0], w_o[0])
    xt, w_gu2, w_down2 = _mixer(
        xt, norm_mix_g, w_conv, w_q, w_k_t, w_v, w_glr, w_gates, w_oc, w_og, w_ob,
        conv_w[0], gate_w_up[0], gate_b[0], gla_norm_g[0], ffn2_w_gu[0], ffn2_w_down[0])
    xt = _ffn2(xt, norm_ffn2_g, w_gu2, w_down2, final_norm_g.reshape(1, d))
    return xt.reshape(bsz, seq, d)
```
